```python
import jax, jax.numpy as jnp
from jax import lax
import numpy as np

D_MODEL = 4096
BATCH = 4
SEQ = 2048
DEPTH = 4
DEC_BATCH = 8
DEC_SEQ = 1
PAST_LEN = 8192
PAGE_SIZE = 128

D_MIX = D_MODEL
ATTN_HEAD_DIM = 128
N_ATTN_HEADS = (D_MIX // 2) // ATTN_HEAD_DIM
RET_HEAD_DIM = 256
N_RET_HEADS = (D_MIX // 2) // RET_HEAD_DIM
D_ATTN = N_ATTN_HEADS * ATTN_HEAD_DIM
D_RET = N_RET_HEADS * RET_HEAD_DIM
SPLITS = [D_ATTN, 2 * D_ATTN, 3 * D_ATTN, 3 * D_ATTN + D_RET, 3 * D_ATTN + 2 * D_RET, 3 * D_ATTN + 3 * D_RET]
N_IN = 3 * D_ATTN + 4 * D_RET
D_FF = ((8 * D_MODEL // 3 + 255) // 256) * 256
DILATIONS = ((128, 1), (512, 4), (2048, 16))
MAX_WINDOW = 2048
DIL_BLOCK = 128
RET_CHUNK = 128
ROPE_THETA = 10000.0
RMS_EPS = 1e-6
NEG_INF = -1e30
STATE_SCALE = 0.5

kernel_name = "hymba_dilated_attn_retention_macaron_step"

F32 = jnp.float32


def rms_norm(x, g):
    xf = x.astype(F32)
    y = xf * lax.rsqrt(jnp.mean(xf * xf, axis=-1, keepdims=True) + RMS_EPS)
    return (y * g.astype(F32)).astype(x.dtype)


def swiglu_half(x, g, w_gate, w_up, w_down):
    h = rms_norm(x, g)
    return x + 0.5 * ((jax.nn.silu(h @ w_gate) * (h @ w_up)) @ w_down)


def rope_inv_freq(d):
    return ROPE_THETA ** (-jnp.arange(0, d, 2, dtype=F32) / d)


def retnet_inv_freq(d):
    return 1.0 / (10000.0 ** jnp.linspace(0.0, 1.0, d // 2, dtype=F32))


def retention_log_decay():
    return jnp.log1p(-jnp.exp2(-5.0 - jnp.arange(N_RET_HEADS, dtype=F32)))


def rotate(x, pos, inv_freq):
    ang = pos.astype(F32)[:, None] * inv_freq[None, :]
    cos = jnp.cos(ang)[None, :, None, :]
    sin = jnp.sin(ang)[None, :, None, :]
    x1, x2 = jnp.split(x, 2, axis=-1)
    return jnp.concatenate([x1 * cos - x2 * sin, x2 * cos + x1 * sin], axis=-1)


def mix_inputs(x, g_mix, w_in, pos, inv_freq_attn, inv_freq_ret):
    b, s, _ = x.shape
    z = rms_norm(x, g_mix) @ w_in
    qa, ka, va, qr, kr, vr, gr = jnp.split(z, SPLITS, axis=-1)
    heads_a = lambda t: t.reshape(b, s, N_ATTN_HEADS, ATTN_HEAD_DIM).astype(F32)
    heads_r = lambda t: t.reshape(b, s, N_RET_HEADS, RET_HEAD_DIM).astype(F32)
    qa = rotate(heads_a(qa), pos, inv_freq_attn) * (ATTN_HEAD_DIM ** -0.5)
    ka = rotate(heads_a(ka), pos, inv_freq_attn)
    qr = rotate(heads_r(qr), pos, inv_freq_ret)
    kr = rotate(heads_r(kr), pos, inv_freq_ret) * (RET_HEAD_DIM ** -0.5)
    return qa, ka, heads_a(va), qr, kr, heads_r(vr), gr.astype(F32)


def mix_output(x, o_attn, o_ret, gate, w_out):
    b, s, _ = x.shape
    o_ret = o_ret * lax.rsqrt(jnp.mean(jnp.square(o_ret), axis=-1, keepdims=True) + RMS_EPS)
    o_ret = o_ret.reshape(b, s, D_RET) * jax.nn.silu(gate)
    o = jnp.concatenate([o_attn.reshape(b, s, D_ATTN), o_ret], axis=-1).astype(x.dtype)
    return x + o @ w_out


def masked_softmax_stats(scores, mask):
    s = jnp.where(mask, scores, NEG_INF)
    m = jnp.max(s, axis=-1, keepdims=True)
    p = jnp.where(mask, jnp.exp(s - m), 0.0)
    l = jnp.sum(p, axis=-1)
    return p / l[..., None], m[..., 0], l


def merge_branches(outs, maxes, sums):
    m = jnp.max(jnp.stack(maxes), axis=0)
    ws = [l * jnp.exp(mi - m) for mi, l in zip(maxes, sums)]
    num = sum(w[..., None] * o for w, o in zip(ws, outs))
    den = sum(ws)
    return num / den[..., None]


def dilated_branch_prompt(q, k, v, window, dilation):
    b, s, h, d = q.shape
    span = window // dilation
    lsub = s // dilation
    nb = -(-lsub // DIL_BLOCK)
    pad = nb * DIL_BLOCK - lsub

    def to_blocks(t):
        t = t.reshape(b, lsub, dilation, h, d).transpose(0, 2, 1, 3, 4)
        t = jnp.pad(t, ((0, 0), (0, 0), (0, pad), (0, 0), (0, 0)))
        return t.reshape(b, dilation, nb, DIL_BLOCK, h, d)

    def with_prev(t):
        prev = jnp.pad(t, ((0, 0), (0, 0), (1, 0), (0, 0), (0, 0), (0, 0)))[:, :, :-1]
        return jnp.concatenate([prev, t], axis=3)

    qb = to_blocks(q)
    kk = with_prev(to_blocks(k))
    vv = with_prev(to_blocks(v))
    scores = jnp.einsum('bcnqhd,bcnkhd->bcnhqk', qb, kk)
    qi = jnp.arange(DIL_BLOCK)[:, None]
    ki = jnp.arange(2 * DIL_BLOCK)[None, :]
    dist = DIL_BLOCK + qi - ki
    blk = jnp.arange(nb)[:, None, None]
    mask = (dist >= 0) & (dist <= span) & ((blk > 0) | (ki >= DIL_BLOCK))[...]
    p, m, l = masked_softmax_stats(scores, mask[:, None])
    o = jnp.einsum('bcnhqk,bcnkhd->bcnqhd', p, vv)
    o = o.reshape(b, dilation, nb * DIL_BLOCK, h, d)[:, :, :lsub].transpose(0, 2, 1, 3, 4).reshape(b, s, h, d)

    def stat_back(t):
        t = t.transpose(0, 1, 2, 4, 3).reshape(b, dilation, nb * DIL_BLOCK, h)[:, :, :lsub]
        return t.transpose(0, 2, 1, 3).reshape(b, s, h)

    return o, stat_back(m), stat_back(l)


def dilated_attention_prompt(q, k, v):
    res = [dilated_branch_prompt(q, k, v, w, r) for (w, r) in DILATIONS]
    return merge_branches([a for a, _, _ in res], [m for _, m, _ in res], [l for _, _, l in res])


def dilated_branch_sample(q, k_all, v_all, n_past, window, dilation):
    t = q.shape[1]
    span = window // dilation
    idx = n_past + jnp.arange(t)[:, None] - dilation * jnp.arange(span + 1)[None, :]
    valid = idx >= 0
    idx = jnp.maximum(idx, 0)
    kg = k_all[:, idx]
    vg = v_all[:, idx]
    scores = jnp.einsum('bthd,btjhd->bthj', q, kg)
    p, m, l = masked_softmax_stats(scores, valid[:, None, :])
    o = jnp.einsum('bthj,btjhd->bthd', p, vg)
    return o, m, l


def dilated_attention_sample(q, k_all, v_all, n_past):
    res = [dilated_branch_sample(q, k_all, v_all, n_past, w, r) for (w, r) in DILATIONS]
    return merge_branches([a for a, _, _ in res], [m for _, m, _ in res], [l for _, _, l in res])


def retention_chunk(state, qkv, log_decay):
    q, k, v = qkv
    c = q.shape[1]
    n = jnp.arange(c, dtype=F32)
    diff = n[:, None] - n[None, :]
    decay = jnp.where(diff >= 0, jnp.exp(jnp.maximum(diff, 0.0)[None] * log_decay[:, None, None]), 0.0)
    inner = jnp.einsum('bnhd,bmhd->bhnm', q, k) * decay[None]
    intra = jnp.einsum('bhnm,bmhe->bnhe', inner, v)
    q_decay = jnp.exp((n[:, None] + 1.0) * log_decay[None, :])
    cross = jnp.einsum('bnhd,bhde->bnhe', q, state) * q_decay[None, :, :, None]
    k_decay = jnp.exp((c - 1.0 - n)[:, None] * log_decay[None, :])
    new_state = (jnp.exp(c * log_decay)[None, :, None, None] * state
                 + jnp.einsum('bmhd,bmhe->bhde', k * k_decay[None, :, :, None], v))
    return new_state, intra + cross


def retention_prompt(q, k, v, log_decay):
    b, s, h, dk = q.shape
    dv = v.shape[-1]
    nc = s // RET_CHUNK
    chunks = lambda t: t.reshape(b, nc, RET_CHUNK, h, t.shape[-1]).swapaxes(0, 1)
    state0 = jnp.zeros((b, h, dk, dv), F32)
    state, o = lax.scan(lambda st, xs: retention_chunk(st, xs, log_decay), state0,
                        (chunks(q), chunks(k), chunks(v)))
    return state, o.swapaxes(0, 1).reshape(b, s, h, dv)


def setup_inputs(seed: int = 0) -> dict:
    key = jax.random.key(seed)
    ks = jax.random.split(key, 17)
    n_buf = min(MAX_WINDOW, PAST_LEN)
    normal = lambda k, shape, scale: jax.random.normal(k, shape, F32) * scale
    gain = lambda k, shape: 1.0 + 0.02 * jax.random.normal(k, shape, F32)
    return {
        "x_prompt": normal(ks[0], (BATCH, SEQ, D_MODEL), 1.0),
        "x_sample": normal(ks[1], (DEC_BATCH, DEC_SEQ, D_MODEL), 1.0),
        "cache_attn_k": normal(ks[2], (DEPTH, DEC_BATCH, n_buf, N_ATTN_HEADS, ATTN_HEAD_DIM), 1.0),
        "cache_attn_v": normal(ks[3], (DEPTH, DEC_BATCH, n_buf, N_ATTN_HEADS, ATTN_HEAD_DIM), 1.0),
        "state_ret": normal(ks[4], (DEPTH, DEC_BATCH, N_RET_HEADS, RET_HEAD_DIM, RET_HEAD_DIM), STATE_SCALE),
        "g_ffn1": gain(ks[5], (DEPTH, D_MODEL)),
        "w_ffn1_gate": normal(ks[6], (DEPTH, D_MODEL, D_FF), D_MODEL ** -0.5),
        "w_ffn1_up": normal(ks[7], (DEPTH, D_MODEL, D_FF), D_MODEL ** -0.5),
        "w_ffn1_down": normal(ks[8], (DEPTH, D_FF, D_MODEL), D_FF ** -0.5),
        "g_mix": gain(ks[9], (DEPTH, D_MODEL)),
        "w_in": normal(ks[10], (DEPTH, D_MODEL, N_IN), D_MODEL ** -0.5),
        "w_out": normal(ks[11], (DEPTH, D_MIX, D_MODEL), D_MIX ** -0.5),
        "g_ffn2": gain(ks[12], (DEPTH, D_MODEL)),
        "w_ffn2_gate": normal(ks[13], (DEPTH, D_MODEL, D_FF), D_MODEL ** -0.5),
        "w_ffn2_up": normal(ks[14], (DEPTH, D_MODEL, D_FF), D_MODEL ** -0.5),
        "w_ffn2_down": normal(ks[15], (DEPTH, D_FF, D_MODEL), D_FF ** -0.5),
        "g_final": gain(ks[16], (D_MODEL,)),
    }


def reference(x_prompt, x_sample, cache_attn_k, cache_attn_v, state_ret,
              g_ffn1, w_ffn1_gate, w_ffn1_up, w_ffn1_down,
              g_mix, w_in, w_out,
              g_ffn2, w_ffn2_gate, w_ffn2_up, w_ffn2_down, g_final):
    seq = x_prompt.shape[1]
    t_new = x_sample.shape[1]
    n_buf = cache_attn_k.shape[2]
    keep = min(MAX_WINDOW, seq)
    pos_p = jnp.arange(seq)
    pos_s = PAST_LEN + jnp.arange(t_new)
    log_decay = retention_log_decay()
    inv_a = rope_inv_freq(ATTN_HEAD_DIM)
    inv_r = retnet_inv_freq(RET_HEAD_DIM)

    xp, xs = x_prompt, x_sample
    pk, pv, ps, sk, sv, ss = [], [], [], [], [], []
    for l in range(DEPTH):
        xp = swiglu_half(xp, g_ffn1[l], w_ffn1_gate[l], w_ffn1_up[l], w_ffn1_down[l])
        xs = swiglu_half(xs, g_ffn1[l], w_ffn1_gate[l], w_ffn1_up[l], w_ffn1_down[l])

        qa, ka, va, qr, kr, vr, gr = mix_inputs(xp, g_mix[l], w_in[l], pos_p, inv_a, inv_r)
        o_attn = dilated_attention_prompt(qa, ka, va)
        st_p, o_ret = retention_prompt(qr, kr, vr, log_decay)
        xp = mix_output(xp, o_attn, o_ret, gr, w_out[l])
        pk.append(ka[:, seq - keep:].astype(x_prompt.dtype))
        pv.append(va[:, seq - keep:].astype(x_prompt.dtype))
        ps.append(st_p.astype(x_prompt.dtype))

        qa, ka, va, qr, kr, vr, gr = mix_inputs(xs, g_mix[l], w_in[l], pos_s, inv_a, inv_r)
        k_all = jnp.concatenate([cache_attn_k[l].astype(F32), ka], axis=1)
        v_all = jnp.concatenate([cache_attn_v[l].astype(F32), va], axis=1)
        o_attn = dilated_attention_sample(qa, k_all, v_all, n_buf)
        st_s, o_ret = retention_chunk(state_ret[l].astype(F32), (qr, kr, vr), log_decay)
        xs = mix_output(xs, o_attn, o_ret, gr, w_out[l])
        sk.append(ka.astype(cache_attn_k.dtype))
        sv.append(va.astype(cache_attn_v.dtype))
        ss.append(st_s.astype(state_ret.dtype))

        xp = swiglu_half(xp, g_ffn2[l], w_ffn2_gate[l], w_ffn2_up[l], w_ffn2_down[l])
        xs = swiglu_half(xs, g_ffn2[l], w_ffn2_gate[l], w_ffn2_up[l], w_ffn2_down[l])

    y_prompt = rms_norm(xp, g_final)
    y_sample = rms_norm(xs, g_final)
    return (y_prompt, y_sample, jnp.stack(pk), jnp.stack(pv), jnp.stack(ps), jnp.stack(sk), jnp.stack(sv), jnp.stack(ss))
```

```python
import functools
import math

import jax
import jax.numpy as jnp
from jax import lax
from jax.experimental import pallas as pl
from jax.experimental.pallas import tpu as pltpu

F32 = jnp.float32
BF16 = jnp.bfloat16

ATTN_HEAD_DIM = 128
RET_HEAD_DIM = 256
DILATIONS = ((128, 1), (512, 4), (2048, 16))
DIL_BLOCK = 128
RET_CHUNK = 128
ROPE_THETA = 10000.0
RMS_EPS = 1e-6
NEG_INF = -1e30
PAST_LEN = 8192

V7X_VMEM_BYTES = 64 * 1024 * 1024
VMEM_HEADROOM_BYTES = 3 * 1024 * 1024
SAMPLE_ROWS = 16


def _vmem_limit(estimate_bytes):
    return int(min(V7X_VMEM_BYTES - VMEM_HEADROOM_BYTES, max(estimate_bytes, 16 * 1024 * 1024)))


def _rms(x, g):
    return x * lax.rsqrt(jnp.mean(x * x, axis=-1, keepdims=True) + RMS_EPS) * g


def _dot(a, b):
    return jnp.dot(a, b, preferred_element_type=F32)


def _dot_nt(a, b):
    return lax.dot_general(a, b, (((1,), (1,)), ((), ())), preferred_element_type=F32)


def _dot_tn(a, b):
    return lax.dot_general(a, b, (((0,), (0,)), ((), ())), preferred_element_type=F32)


def _ffn_kernel(x_hbm, g_ref, wg_ref, wu_ref, wd_ref, gn_ref, *rest, bm, mode, row_chunk, col_chunk):
    if mode == "norm":
        y_hbm, hn_hbm, acc, h_scr, sem = rest
    else:
        y_hbm, acc, h_scr, sem = rest
    i = pl.program_id(0)
    j = pl.program_id(1)
    nj = pl.num_programs(1)
    rows = pl.ds(pl.multiple_of(i * bm, bm), bm)
    d = acc.shape[1]
    n_row_chunks = bm // row_chunk

    def for_row_chunks(fn):
        def body(t, carry):
            fn(pl.ds(pl.multiple_of(t * row_chunk, row_chunk), row_chunk))
            return carry
        lax.fori_loop(0, n_row_chunks, body, 0)

    @pl.when(j == 0)
    def _load():
        cp = pltpu.make_async_copy(x_hbm.at[rows], acc, sem.at[0])
        cp.start()
        cp.wait()

        def norm_rows(r):
            h_scr[r, :] = _rms(acc[r, :], g_ref[...]).astype(BF16)
        for_row_chunks(norm_rows)

    h = h_scr[...]
    gate = _dot(h, wg_ref[...].astype(BF16))
    up = _dot(h, wu_ref[...].astype(BF16))
    a = ((0.5 * gate) * jax.nn.sigmoid(gate) * up).astype(BF16)
    for c0 in range(0, d, col_chunk):
        acc[:, c0:c0 + col_chunk] += _dot(a, wd_ref[:, c0:c0 + col_chunk].astype(BF16))

    @pl.when(j == nj - 1)
    def _store():
        if mode == "final":
            def fin_rows(r):
                acc[r, :] = _rms(acc[r, :], gn_ref[...])
            for_row_chunks(fin_rows)
            cp = pltpu.make_async_copy(acc, y_hbm.at[rows], sem.at[0])
            cp.start()
            cp.wait()
        else:
            cp = pltpu.make_async_copy(acc, y_hbm.at[rows], sem.at[0])
            cp.start()
            if mode == "norm":
                def nrm_rows(r):
                    h_scr[r, :] = _rms(acc[r, :], gn_ref[...]).astype(BF16)
                for_row_chunks(nrm_rows)
                cp2 = pltpu.make_async_copy(h_scr, hn_hbm.at[rows], sem.at[1])
                cp2.start()
                cp2.wait()
            cp.wait()


def _ffn(x, g, wg, wu, wd, gn, *, mode, bm, bn=256):
    m, d = x.shape
    dff = wg.shape[1]
    assert m % bm == 0 and dff % bn == 0
    row_chunk = min(bm, 64)
    col_chunk = min(d, 512)
    any_spec = pl.BlockSpec(memory_space=pl.ANY)
    out_shape = [jax.ShapeDtypeStruct((m, d), F32)]
    out_specs = [any_spec]
    if mode == "norm":
        out_shape.append(jax.ShapeDtypeStruct((m, d), BF16))
        out_specs.append(any_spec)
    est = bm * d * 6 + 3 * 2 * d * bn * 4 + 3 * d * bn * 2 + 4 * bm * bn * 4 + bm * col_chunk * 4
    res = pl.pallas_call(
        functools.partial(_ffn_kernel, bm=bm, mode=mode, row_chunk=row_chunk, col_chunk=col_chunk),
        grid=(m // bm, dff // bn),
        in_specs=[
            any_spec,
            pl.BlockSpec((1, d), lambda i, j: (0, 0)),
            pl.BlockSpec((d, bn), lambda i, j: (0, j)),
            pl.BlockSpec((d, bn), lambda i, j: (0, j)),
            pl.BlockSpec((bn, d), lambda i, j: (j, 0)),
            pl.BlockSpec((1, d), lambda i, j: (0, 0)),
        ],
        out_specs=out_specs,
        out_shape=out_shape,
        scratch_shapes=[
            pltpu.VMEM((bm, d), F32),
            pltpu.VMEM((bm, d), BF16),
            pltpu.SemaphoreType.DMA((2,)),
        ],
        compiler_params=pltpu.CompilerParams(
            dimension_semantics=("arbitrary", "arbitrary"),
            vmem_limit_bytes=_vmem_limit(est + (4 << 20))),
        name=f"ffn_{mode}_{bm}",
    )(x, g.reshape(1, d), wg, wu, wd, gn.reshape(1, d))
    return res if mode == "norm" else res[0]


def _proj_kernel(h_ref, w_ref, c_ref, s_ref, o_ref, w_scr, *, hd):
    i = pl.program_id(1)

    @pl.when(i == 0)
    def _cast():
        w_scr[...] = w_ref[...].astype(BF16)

    z = _dot(h_ref[...], w_scr[...])
    cc = c_ref[0]
    ss = s_ref[0]
    bn = z.shape[1]
    for c0 in range(0, bn, hd):
        zh = z[:, c0:c0 + hd]
        swapped = pltpu.roll(zh, hd // 2, axis=1)
        o_ref[:, c0:c0 + hd] = zh * cc + swapped * ss


def _proj(h, w, ctab, stab, *, col0, n, hd, seg, bm, bn=512):
    m, d = h.shape
    p = ctab.shape[1]
    assert m % bm == 0 and n % bn == 0 and col0 % bn == 0 and seg % bn == 0 and (p % bm == 0 or p == m)
    npb = max(p // bm, 1)
    cb0 = col0 // bn
    spc = seg // bn
    est = 2 * bm * d * 2 + 2 * d * bn * 4 + d * bn * 2 + 4 * bm * bn * 4 + 4 * bm * hd * 4
    return pl.pallas_call(
        functools.partial(_proj_kernel, hd=hd),
        grid=(n // bn, m // bm),
        in_specs=[
            pl.BlockSpec((bm, d), lambda j, i: (i, 0)),
            pl.BlockSpec((d, bn), lambda j, i: (0, cb0 + j)),
            pl.BlockSpec((1, bm, hd), lambda j, i: (j // spc, i % npb, 0)),
            pl.BlockSpec((1, bm, hd), lambda j, i: (j // spc, i % npb, 0)),
        ],
        out_specs=pl.BlockSpec((bm, bn), lambda j, i: (i, j)),
        out_shape=jax.ShapeDtypeStruct((m, n), F32),
        scratch_shapes=[pltpu.VMEM((d, bn), BF16)],
        compiler_params=pltpu.CompilerParams(
            dimension_semantics=("arbitrary", "arbitrary"),
            vmem_limit_bytes=_vmem_limit(est + (4 << 20))),
        name=f"proj_rot_hd{hd}_{bm}",
    )(h, w, ctab, stab)


def _out_kernel(x_ref, oa_ref, or_ref, w_ref, y_ref, w_scr):
    i = pl.program_id(1)

    @pl.when(i == 0)
    def _cast():
        w_scr[...] = w_ref[...].astype(BF16)

    ka = oa_ref.shape[1]
    y_ref[...] = x_ref[...] + (_dot(oa_ref[...], w_scr[:ka, :]) + _dot(or_ref[...], w_scr[ka:, :]))


def _out_proj(x, oa, orr, w, *, bm, bn=512):
    m, d = x.shape
    ka, kr = oa.shape[1], orr.shape[1]
    assert m % bm == 0 and d % bn == 0 and w.shape[0] == ka + kr
    est = 2 * bm * (ka + kr) * 2 + 2 * (ka + kr) * bn * 4 + (ka + kr) * bn * 2 + 6 * bm * bn * 4
    return pl.pallas_call(
        _out_kernel,
        grid=(d // bn, m // bm),
        in_specs=[
            pl.BlockSpec((bm, bn), lambda j, i: (i, j)),
            pl.BlockSpec((bm, ka), lambda j, i: (i, 0)),
            pl.BlockSpec((bm, kr), lambda j, i: (i, 0)),
            pl.BlockSpec((ka + kr, bn), lambda j, i: (0, j)),
        ],
        out_specs=pl.BlockSpec((bm, bn), lambda j, i: (i, j)),
        out_shape=jax.ShapeDtypeStruct((m, d), F32),
        scratch_shapes=[pltpu.VMEM((ka + kr, bn), BF16)],
        compiler_params=pltpu.CompilerParams(
            dimension_semantics=("arbitrary", "arbitrary"),
            vmem_limit_bytes=_vmem_limit(est + (4 << 20))),
        name=f"out_proj_{bm}",
    )(x, oa, orr, w)


def _attn_block(qb, kw, vw, first):
    s = _dot_nt(qb, kw)
    qi = lax.broadcasted_iota(jnp.int32, s.shape, 0)
    kj = lax.broadcasted_iota(jnp.int32, s.shape, 1)
    if first:
        mask = kj <= qi
    else:
        mask = (kj >= qi) & (kj <= qi + DIL_BLOCK)
    s = jnp.where(mask, s, NEG_INF)
    m = jnp.max(s, axis=-1, keepdims=True)
    p = jnp.where(mask, jnp.exp(s - m), 0.0)
    l = jnp.sum(p, axis=-1, keepdims=True)
    u = _dot(p.astype(BF16), vw)
    return u, m, l


def _attn_kernel(q_ref, k_ref, v_ref, o_ref, u_scr, m_scr, l_scr, *, seq):
    blk = DIL_BLOCK
    hd = q_ref.shape[2]

    def load(ref, start, size, stride):
        if stride == 1:
            return ref[0, pl.ds(start, size), :].astype(BF16)
        return ref[0, pl.ds(start, size, stride=stride), :].astype(BF16)

    def block(r, c, n):
        q0 = c + r * n * blk
        qb = load(q_ref, q0, blk, r)
        if n == 0:
            kw = load(k_ref, q0, blk, r)
            vw = load(v_ref, q0, blk, r)
        else:
            k0 = c + r * (n - 1) * blk
            kw = load(k_ref, k0, 2 * blk, r)
            vw = load(v_ref, k0, 2 * blk, r)
        u, m, l = _attn_block(qb, kw, vw, n == 0)
        if r == 1:
            rows = pl.ds(q0, blk)
        else:
            rows = pl.ds(q0, blk, stride=r)
        return rows, u, m, l

    first_r = DILATIONS[0][1]
    last_r = DILATIONS[-1][1]
    for (_, r) in DILATIONS:
        lsub = seq // r
        nb = -(-lsub // blk)
        for c in range(r):
            for n in range(nb):
                rows, u, m, l = block(r, c, n)
                m = jnp.broadcast_to(m, (blk, hd))
                l = jnp.broadcast_to(l, (blk, hd))
                if r == first_r:
                    m_run, l_run, u_run = m, l, u
                else:
                    m_old = m_scr[rows, :]
                    m_run = jnp.maximum(m_old, m)
                    a_old = jnp.exp(m_old - m_run)
                    a_new = jnp.exp(m - m_run)
                    l_run = a_old * l_scr[rows, :] + a_new * l
                    u_run = a_old * u_scr[rows, :] + a_new * u
                if r == last_r:
                    u_scr[rows, :] = u_run / l_run
                else:
                    u_scr[rows, :] = u_run
                    m_scr[rows, :] = m_run
                    l_scr[rows, :] = l_run
    o_ref[0] = u_scr[...].astype(BF16)


def _attention(za, *, batch, seq, n_heads):
    hd = ATTN_HEAD_DIM
    z3 = za.reshape(batch, seq, 3 * n_heads * hd)
    est = 3 * 2 * seq * hd * 4 + 2 * seq * hd * 2 + 3 * seq * hd * 4
    o = pl.pallas_call(
        functools.partial(_attn_kernel, seq=seq),
        grid=(batch, n_heads),
        in_specs=[
            pl.BlockSpec((1, seq, hd), lambda b, h: (b, 0, h)),
            pl.BlockSpec((1, seq, hd), lambda b, h: (b, 0, n_heads + h)),
            pl.BlockSpec((1, seq, hd), lambda b, h: (b, 0, 2 * n_heads + h)),
        ],
        out_specs=pl.BlockSpec((1, seq, hd), lambda b, h: (b, 0, h)),
        out_shape=jax.ShapeDtypeStruct((batch, seq, n_heads * hd), BF16),
        scratch_shapes=[pltpu.VMEM((seq, hd), F32)] * 3,
        compiler_params=pltpu.CompilerParams(
            dimension_semantics=("arbitrary", "arbitrary"),
            vmem_limit_bytes=_vmem_limit(est + (32 << 20))),
        name="dilated_attn_prompt",
    )(z3, z3, z3)
    return o.reshape(batch * seq, n_heads * hd)


def _ret_kernel(gpow_ref, q_ref, k_ref, v_ref, g_ref, dec_ref, qd_ref, kd_ref, o_ref, st_ref, *, seq):
    ck = RET_CHUNK
    h = pl.program_id(1)
    gpow = gpow_ref[h]
    st_ref[...] = jnp.zeros_like(st_ref)

    def chunk(t, carry):
        rows = pl.ds(pl.multiple_of(t * ck, ck), ck)
        q = q_ref[0, rows, :].astype(BF16)
        k = k_ref[0, rows, :]
        v = v_ref[0, rows, :].astype(BF16)
        inner = _dot_nt(q, k.astype(BF16)) * dec_ref[0]
        intra = _dot(inner.astype(BF16), v)
        st = st_ref[0, 0]
        cross = _dot(q, st.astype(BF16)) * qd_ref[0]
        o = intra + cross
        kdec = (k * kd_ref[0]).astype(BF16)
        st_ref[0, 0] = gpow * st + _dot_tn(kdec, v)
        o = o * lax.rsqrt(jnp.mean(o * o, axis=-1, keepdims=True) + RMS_EPS)
        g = g_ref[0, rows, :]
        o_ref[0, rows, :] = (o * (g * jax.nn.sigmoid(g))).astype(BF16)
        return carry

    lax.fori_loop(0, seq // ck, chunk, 0)


def _ret_tables(n_heads, hd):
    hh = jnp.arange(n_heads, dtype=F32)
    ld = jnp.log1p(-jnp.exp2(-5.0 - hh))
    n = jnp.arange(RET_CHUNK, dtype=F32)
    diff = n[:, None] - n[None, :]
    dec = jnp.where(diff >= 0, jnp.exp(jnp.maximum(diff, 0.0)[None] * ld[:, None, None]), 0.0)
    qd = jnp.exp((n[None, :] + 1.0) * ld[:, None])
    kd = jnp.exp((RET_CHUNK - 1.0 - n)[None, :] * ld[:, None])
    qd = jnp.broadcast_to(qd[:, :, None], (n_heads, RET_CHUNK, hd))
    kd = jnp.broadcast_to(kd[:, :, None], (n_heads, RET_CHUNK, hd))
    gpow = jnp.exp(RET_CHUNK * ld)
    return ld, dec, qd, kd, gpow


def _retention(zr, *, batch, seq, n_heads):
    hd = RET_HEAD_DIM
    z3 = zr.reshape(batch, seq, 4 * n_heads * hd)
    _, dec, qd, kd, gpow = _ret_tables(n_heads, hd)
    blk = lambda off: pl.BlockSpec((1, seq, hd), lambda b, h: (b, 0, off + h))
    tab = lambda r, c: pl.BlockSpec((1, r, c), lambda b, h: (h, 0, 0))
    est = 4 * 2 * seq * hd * 4 + 2 * seq * hd * 2 + 2 * hd * hd * 4
    o, st = pl.pallas_call(
        functools.partial(_ret_kernel, seq=seq),
        grid=(batch, n_heads),
        in_specs=[
            pl.BlockSpec(memory_space=pltpu.SMEM),
            blk(0), blk(n_heads), blk(2 * n_heads), blk(3 * n_heads),
            tab(RET_CHUNK, RET_CHUNK), tab(RET_CHUNK, hd), tab(RET_CHUNK, hd),
        ],
        out_specs=[
            pl.BlockSpec((1, seq, hd), lambda b, h: (b, 0, h)),
            pl.BlockSpec((1, 1, hd, hd), lambda b, h: (b, h, 0, 0)),
        ],
        out_shape=[
            jax.ShapeDtypeStruct((batch, seq, n_heads * hd), BF16),
            jax.ShapeDtypeStruct((batch, n_heads, hd, hd), F32),
        ],
        compiler_params=pltpu.CompilerParams(
            dimension_semantics=("arbitrary", "arbitrary"),
            vmem_limit_bytes=_vmem_limit(est + (32 << 20))),
        name="retention_prompt",
    )(gpow, z3, z3, z3, z3, dec, qd, kd)
    return o.reshape(batch * seq, n_heads * hd), st


def _bf(x):
    return x.astype(BF16).astype(F32)


def _sample_kernel(gam_ref, za_ref, zr_ref, k1_ref, k4_ref, k16_ref, v1_ref, v4_ref, v16_ref, st_ref,
                   o_ref, ns_ref, *, n_ah, n_rh):
    ahd, rhd = ATTN_HEAD_DIM, RET_HEAD_DIM
    da = n_ah * ahd
    dr = n_rh * rhd
    za = za_ref[0]
    kbufs = (k1_ref, k4_ref, k16_ref)
    vbufs = (v1_ref, v4_ref, v16_ref)
    for h in range(n_ah):
        cs = slice(h * ahd, (h + 1) * ahd)
        q = za[:, h * ahd:(h + 1) * ahd]
        kn = za[:, da + h * ahd: da + (h + 1) * ahd]
        vn = za[:, 2 * da + h * ahd: 2 * da + (h + 1) * ahd]
        q8 = jnp.broadcast_to(q, (8, ahd)).astype(BF16)
        s_new = jnp.sum(_bf(q) * _bf(kn), axis=-1, keepdims=True)
        vnb = _bf(vn)
        m_run = l_run = u_run = None
        for kb, vb in zip(kbufs, vbufs):
            s = _dot_nt(q8, kb[0, :, cs].astype(BF16))[:1]
            m = jnp.maximum(jnp.max(s, axis=-1, keepdims=True), s_new)
            p = jnp.exp(s - m)
            pn = jnp.exp(s_new - m)
            l = jnp.sum(p, axis=-1, keepdims=True) + pn
            p8 = jnp.broadcast_to(p, (8, p.shape[1])).astype(BF16)
            u = _dot(p8, vb[0, :, cs].astype(BF16))[:1] + pn * vnb
            if m_run is None:
                m_run, l_run, u_run = m, l, u
            else:
                m_new = jnp.maximum(m_run, m)
                a_old = jnp.exp(m_run - m_new)
                a_new = jnp.exp(m - m_new)
                l_run = a_old * l_run + a_new * l
                u_run = a_old * u_run + a_new * u
                m_run = m_new
        o_ref[0, :, h * ahd:(h + 1) * ahd] = u_run / l_run

    zr = zr_ref[0]
    row0 = lax.broadcasted_iota(jnp.int32, (8, rhd), 0) == 0
    for h in range(n_rh):
        gam = gam_ref[h]
        q = zr[:, h * rhd:(h + 1) * rhd]
        k = zr[:, dr + h * rhd: dr + (h + 1) * rhd]
        v = zr[:, 2 * dr + h * rhd: 2 * dr + (h + 1) * rhd]
        g = zr[:, 3 * dr + h * rhd: 3 * dr + (h + 1) * rhd]
        st = st_ref[0, h]
        q8 = jnp.broadcast_to(q, (8, rhd)).astype(BF16)
        cross = _dot(q8, st.astype(BF16))[:1] * gam
        inner = _bf(jnp.sum(_bf(q) * _bf(k), axis=-1, keepdims=True))
        o = inner * _bf(v) + cross
        k8 = jnp.where(row0, jnp.broadcast_to(k, (8, rhd)), 0.0).astype(BF16)
        v8 = jnp.broadcast_to(v, (8, rhd)).astype(BF16)
        ns_ref[0, h] = gam * st + _dot_tn(k8, v8)
        o = o * lax.rsqrt(jnp.mean(o * o, axis=-1, keepdims=True) + RMS_EPS)
        o_ref[0, :, da + h * rhd: da + (h + 1) * rhd] = o * (g * jax.nn.sigmoid(g))


def _sample_mixer(za, zr, cache_k, cache_v, state, layer, *, n_ah, n_rh):
    nl, nb_, n_buf = cache_k.shape[:3]
    da = n_ah * ATTN_HEAD_DIM
    dr = n_rh * RET_HEAD_DIM
    assert n_buf == DILATIONS[-1][0]
    span = DIL_BLOCK
    views, specs = [], []
    for cache in (cache_k, cache_v):
        for (_, r) in DILATIONS:
            views.append(cache.reshape(nl * nb_, n_buf // r, r * da))
            rb = (n_buf // r) // span - 1
            specs.append(pl.BlockSpec((1, span, da), lambda b, rb=rb: (layer * nb_ + b, rb, 0)))
    ld = jnp.log1p(-jnp.exp2(-5.0 - jnp.arange(n_rh, dtype=F32)))
    gam = jnp.exp(ld)
    st4 = state.reshape(nl * nb_, n_rh, RET_HEAD_DIM, RET_HEAD_DIM)
    est = 2 * 6 * span * da * 4 + 4 * n_rh * RET_HEAD_DIM * RET_HEAD_DIM * 4
    o, ns = pl.pallas_call(
        functools.partial(_sample_kernel, n_ah=n_ah, n_rh=n_rh),
        grid=(nb_,),
        in_specs=[
            pl.BlockSpec(memory_space=pltpu.SMEM),
            pl.BlockSpec((1, 1, 3 * da), lambda b: (b, 0, 0)),
            pl.BlockSpec((1, 1, 4 * dr), lambda b: (b, 0, 0)),
            *specs,
            pl.BlockSpec((1, n_rh, RET_HEAD_DIM, RET_HEAD_DIM), lambda b: (layer * nb_ + b, 0, 0, 0)),
        ],
        out_specs=[
            pl.BlockSpec((1, 1, da + dr), lambda b: (b, 0, 0)),
            pl.BlockSpec((1, n_rh, RET_HEAD_DIM, RET_HEAD_DIM), lambda b: (b, 0, 0, 0)),
        ],
        out_shape=[
            jax.ShapeDtypeStruct((nb_, 1, da + dr), F32),
            jax.ShapeDtypeStruct((nb_, n_rh, RET_HEAD_DIM, RET_HEAD_DIM), F32),
        ],
        compiler_params=pltpu.CompilerParams(
            dimension_semantics=("arbitrary",),
            vmem_limit_bytes=_vmem_limit(est + (32 << 20))),
        name="sample_mixer",
    )(gam, za.reshape(nb_, 1, 3 * da), zr.reshape(nb_, 1, 4 * dr), *views, st4)
    return o.reshape(nb_, da + dr), ns


def _rot_tables(pos, inv_freq, scales):
    ang = pos.astype(F32)[:, None] * inv_freq[None, :]
    cos, sin = jnp.cos(ang), jnp.sin(ang)
    c_full = jnp.concatenate([cos, cos], axis=-1)
    s_full = jnp.concatenate([-sin, sin], axis=-1)
    cs, ss = [], []
    for sc in scales:
        if sc is None:
            cs.append(jnp.ones_like(c_full))
            ss.append(jnp.zeros_like(s_full))
        else:
            cs.append(c_full * sc)
            ss.append(s_full * sc)
    return jnp.stack(cs), jnp.stack(ss)


def _tables(pos):
    inv_a = ROPE_THETA ** (-jnp.arange(0, ATTN_HEAD_DIM, 2, dtype=F32) / ATTN_HEAD_DIM)
    inv_r = 1.0 / (10000.0 ** jnp.linspace(0.0, 1.0, RET_HEAD_DIM // 2, dtype=F32))
    ta = _rot_tables(pos, inv_a, (ATTN_HEAD_DIM ** -0.5, 1.0, None))
    tr = _rot_tables(pos, inv_r, (1.0, RET_HEAD_DIM ** -0.5, None, None))
    return ta, tr


def kernel(x_prompt, x_sample, cache_attn_k, cache_attn_v, state_ret, g_ffn1, w_ffn1_gate, w_ffn1_up,
           w_ffn1_down, g_mix, w_in, w_out, g_ffn2, w_ffn2_gate, w_ffn2_up, w_ffn2_down, g_final):
    batch, seq, d = x_prompt.shape
    dec_batch, t_new, _ = x_sample.shape
    assert t_new == 1 and dec_batch <= SAMPLE_ROWS
    depth = g_ffn1.shape[0]
    d_attn = d // 2
    d_ret = d // 2
    n_ah = d_attn // ATTN_HEAD_DIM
    n_rh = d_ret // RET_HEAD_DIM
    m = batch * seq
    bm = min(1024, seq)
    ms = SAMPLE_ROWS

    (ca_p, sa_p), (cr_p, sr_p) = _tables(jnp.arange(seq))
    (ca_s, sa_s), (cr_s, sr_s) = _tables(jnp.full((ms,), PAST_LEN))

    xp = x_prompt.reshape(m, d)
    xs = jnp.pad(x_sample.reshape(dec_batch, d), ((0, ms - dec_batch), (0, 0)))
    pk, pv, ps, sk, sv, ss = [], [], [], [], [], []
    yp = ys = None
    for l in range(depth):
        ffn1 = (g_ffn1[l], w_ffn1_gate[l], w_ffn1_up[l], w_ffn1_down[l], g_mix[l])
        ffn2_mode = "final" if l == depth - 1 else "mid"
        ffn2 = (g_ffn2[l], w_ffn2_gate[l], w_ffn2_up[l], w_ffn2_down[l], g_final)

        xp, hp = _ffn(xp, *ffn1, mode="norm", bm=bm)
        za = _proj(hp, w_in[l], ca_p, sa_p, col0=0, n=3 * d_attn, hd=ATTN_HEAD_DIM, seg=d_attn, bm=bm)
        zr = _proj(hp, w_in[l], cr_p, sr_p, col0=3 * d_attn, n=4 * d_ret, hd=RET_HEAD_DIM, seg=d_ret, bm=bm)
        oa = _attention(za, batch=batch, seq=seq, n_heads=n_ah)
        orr, st_p = _retention(zr, batch=batch, seq=seq, n_heads=n_rh)
        xp = _out_proj(xp, oa, orr, w_out[l], bm=bm)
        keep = min(DILATIONS[-1][0], seq)
        za4 = za.reshape(batch, seq, 3, n_ah, ATTN_HEAD_DIM)
        pk.append(za4[:, seq - keep:, 1])
        pv.append(za4[:, seq - keep:, 2])
        ps.append(st_p)
        res = _ffn(xp, *ffn2, mode=ffn2_mode, bm=bm)
        if l == depth - 1:
            yp = res
        else:
            xp = res

        xs, hs = _ffn(xs, *ffn1, mode="norm", bm=ms)
        zas = _proj(hs, w_in[l], ca_s, sa_s, col0=0, n=3 * d_attn, hd=ATTN_HEAD_DIM, seg=d_attn, bm=ms)
        zrs = _proj(hs, w_in[l], cr_s, sr_s, col0=3 * d_attn, n=4 * d_ret, hd=RET_HEAD_DIM, seg=d_ret, bm=ms)
        o_s, st_s = _sample_mixer(zas[:dec_batch], zrs[:dec_batch], cache_attn_k, cache_attn_v, state_ret, l,
                                  n_ah=n_ah, n_rh=n_rh)
        o_s = jnp.pad(o_s, ((0, ms - dec_batch), (0, 0))).astype(BF16)
        xs = _out_proj(xs, o_s[:, :d_attn], o_s[:, d_attn:], w_out[l], bm=ms)
        zas4 = zas[:dec_batch].reshape(dec_batch, 1, 3, n_ah, ATTN_HEAD_DIM)
        sk.append(zas4[:, :, 1])
        sv.append(zas4[:, :, 2])
        ss.append(st_s)
        res = _ffn(xs, *ffn2, mode=ffn2_mode, bm=ms)
        if l == depth - 1:
            ys = res
        else:
            xs = res

    y_prompt = yp.reshape(batch, seq, d)
    y_sample = ys[:dec_batch].reshape(dec_batch, t_new, d)
    return (y_prompt, y_sample, jnp.stack(pk), jnp.stack(pv), jnp.stack(ps),
            jnp.stack(sk), jnp.stack(sv), jnp.stack(ss))
```

```python
import functools

import jax
import jax.numpy as jnp
from jax import lax
from jax.experimental import pallas as pl
from jax.experimental.pallas import tpu as pltpu

F32 = jnp.float32
BF16 = jnp.bfloat16

ATTN_HEAD_DIM = 128
RET_HEAD_DIM = 256
DILATIONS = ((128, 1), (512, 4), (2048, 16))
DIL_BLOCK = 128
RET_CHUNK = 128
ROPE_THETA = 10000.0
RMS_EPS = 1e-6
NEG_INF = -1e30
PAST_LEN = 8192

V7X_VMEM_BYTES = 64 * 1024 * 1024
VMEM_HEADROOM_BYTES = 3 * 1024 * 1024
SAMPLE_ROWS = 16


def _vmem_limit(estimate_bytes):
    return int(min(V7X_VMEM_BYTES - VMEM_HEADROOM_BYTES, max(estimate_bytes, 16 * 1024 * 1024)))


def _rms(x, g):
    return x * lax.rsqrt(jnp.mean(x * x, axis=-1, keepdims=True) + RMS_EPS) * g


def _dot(a, b):
    return jnp.dot(a, b, preferred_element_type=F32)


def _dot_nt(a, b):
    return lax.dot_general(a, b, (((1,), (1,)), ((), ())), preferred_element_type=F32)


def _dot_tn(a, b):
    return lax.dot_general(a, b, (((0,), (0,)), ((), ())), preferred_element_type=F32)


def _ffn_kernel(x_hbm, xs_ref, g_ref, wg_ref, wu_ref, wd_ref, gn_ref, *rest, bm, mode, row_chunk, col_chunk):
    if mode == "norm":
        y_hbm, hn_hbm, ys_ref, hns_ref, acc, h_scr, hs_scr, sem = rest
    else:
        y_hbm, ys_ref, acc, h_scr, hs_scr, sem = rest
    i = pl.program_id(0)
    j = pl.program_id(1)
    nj = pl.num_programs(1)
    rows = pl.ds(pl.multiple_of(i * bm, bm), bm)
    d = acc.shape[1]
    n_row_chunks = bm // row_chunk

    def for_row_chunks(fn):
        def body(t, carry):
            fn(pl.ds(pl.multiple_of(t * row_chunk, row_chunk), row_chunk))
            return carry
        lax.fori_loop(0, n_row_chunks, body, 0)

    @pl.when(j == 0)
    def _load():
        cp = pltpu.make_async_copy(x_hbm.at[rows], acc, sem.at[0])
        cp.start()
        cp.wait()

        def norm_rows(r):
            h_scr[r, :] = _rms(acc[r, :], g_ref[...]).astype(BF16)
        for_row_chunks(norm_rows)

    wg = wg_ref[...].astype(BF16)
    wu = wu_ref[...].astype(BF16)
    wd = wd_ref[...].astype(BF16)

    def half_step(h):
        gate = _dot(h, wg)
        up = _dot(h, wu)
        return ((0.5 * gate) * jax.nn.sigmoid(gate) * up).astype(BF16)

    a = half_step(h_scr[...])
    for c0 in range(0, d, col_chunk):
        acc[:, c0:c0 + col_chunk] += _dot(a, wd[:, c0:c0 + col_chunk])

    @pl.when(i == 0)
    def _sample_rows():
        @pl.when(j == 0)
        def _():
            xs = xs_ref[...]
            ys_ref[...] = xs
            hs_scr[...] = _rms(xs, g_ref[...]).astype(BF16)

        ys_ref[...] += _dot(half_step(hs_scr[...]), wd)

        @pl.when(j == nj - 1)
        def _():
            if mode == "final":
                ys_ref[...] = _rms(ys_ref[...], gn_ref[...])
            elif mode == "norm":
                hns_ref[...] = _rms(ys_ref[...], gn_ref[...]).astype(BF16)

    @pl.when(j == nj - 1)
    def _store():
        if mode == "final":
            def fin_rows(r):
                acc[r, :] = _rms(acc[r, :], gn_ref[...])
            for_row_chunks(fin_rows)
            cp = pltpu.make_async_copy(acc, y_hbm.at[rows], sem.at[0])
            cp.start()
            cp.wait()
        else:
            cp = pltpu.make_async_copy(acc, y_hbm.at[rows], sem.at[0])
            cp.start()
            if mode == "norm":
                def nrm_rows(r):
                    h_scr[r, :] = _rms(acc[r, :], gn_ref[...]).astype(BF16)
                for_row_chunks(nrm_rows)
                cp2 = pltpu.make_async_copy(h_scr, hn_hbm.at[rows], sem.at[1])
                cp2.start()
                cp2.wait()
            cp.wait()


def _ffn(x, xs, g, wg, wu, wd, gn, *, mode, bm, bn=256):
    m, d = x.shape
    ms = xs.shape[0]
    dff = wg.shape[1]
    assert m % bm == 0 and dff % bn == 0
    row_chunk = min(bm, 64)
    col_chunk = min(d, 512)
    any_spec = pl.BlockSpec(memory_space=pl.ANY)
    small = pl.BlockSpec((ms, d), lambda i, j: (0, 0))
    vec = pl.BlockSpec((1, d), lambda i, j: (0, 0))
    out_shape = [jax.ShapeDtypeStruct((m, d), F32)]
    out_specs = [any_spec]
    if mode == "norm":
        out_shape.append(jax.ShapeDtypeStruct((m, d), BF16))
        out_specs.append(any_spec)
    out_shape.append(jax.ShapeDtypeStruct((ms, d), F32))
    out_specs.append(small)
    if mode == "norm":
        out_shape.append(jax.ShapeDtypeStruct((ms, d), BF16))
        out_specs.append(small)
    est = bm * d * 6 + 3 * 2 * d * bn * 4 + 3 * d * bn * 2 + 4 * bm * bn * 4 + bm * col_chunk * 4
    return pl.pallas_call(
        functools.partial(_ffn_kernel, bm=bm, mode=mode, row_chunk=row_chunk, col_chunk=col_chunk),
        grid=(m // bm, dff // bn),
        in_specs=[
            any_spec,
            small,
            vec,
            pl.BlockSpec((d, bn), lambda i, j: (0, j)),
            pl.BlockSpec((d, bn), lambda i, j: (0, j)),
            pl.BlockSpec((bn, d), lambda i, j: (j, 0)),
            vec,
        ],
        out_specs=out_specs,
        out_shape=out_shape,
        scratch_shapes=[
            pltpu.VMEM((bm, d), F32),
            pltpu.VMEM((bm, d), BF16),
            pltpu.VMEM((ms, d), BF16),
            pltpu.SemaphoreType.DMA((2,)),
        ],
        compiler_params=pltpu.CompilerParams(
            dimension_semantics=("arbitrary", "arbitrary"),
            vmem_limit_bytes=_vmem_limit(est + (4 << 20))),
        name=f"ffn_{mode}",
    )(x, xs, g.reshape(1, d), wg, wu, wd, gn.reshape(1, d))


def _proj_kernel(h_ref, hs_ref, w_ref, c_ref, s_ref, cs_ref, ss_ref, o_ref, os_ref, w_scr, *, hd):
    i = pl.program_id(1)

    def rotate_store(z, cc, ss, out):
        for c0 in range(0, z.shape[1], hd):
            zh = z[:, c0:c0 + hd]
            out[:, c0:c0 + hd] = zh * cc + pltpu.roll(zh, hd // 2, axis=1) * ss

    @pl.when(i == 0)
    def _first():
        w_scr[...] = w_ref[...].astype(BF16)
        rotate_store(_dot(hs_ref[...], w_scr[...]), cs_ref[0], ss_ref[0], os_ref)

    rotate_store(_dot(h_ref[...], w_scr[...]), c_ref[0], s_ref[0], o_ref)


def _proj(h, hs, w, tabs, tabs_s, *, col0, n, hd, seg, bm, bn=512):
    m, d = h.shape
    ms = hs.shape[0]
    ctab, stab = tabs
    p = ctab.shape[1]
    assert m % bm == 0 and n % bn == 0 and col0 % bn == 0 and seg % bn == 0 and p % bm == 0
    npb = p // bm
    cb0 = col0 // bn
    spc = seg // bn
    est = 2 * bm * d * 2 + 2 * d * bn * 4 + d * bn * 2 + 4 * bm * bn * 4 + 4 * bm * hd * 4
    tab = pl.BlockSpec((1, bm, hd), lambda j, i: (j // spc, i % npb, 0))
    tab_s = pl.BlockSpec((1, ms, hd), lambda j, i: (j // spc, 0, 0))
    return pl.pallas_call(
        functools.partial(_proj_kernel, hd=hd),
        grid=(n // bn, m // bm),
        in_specs=[
            pl.BlockSpec((bm, d), lambda j, i: (i, 0)),
            pl.BlockSpec((ms, d), lambda j, i: (0, 0)),
            pl.BlockSpec((d, bn), lambda j, i: (0, cb0 + j)),
            tab, tab, tab_s, tab_s,
        ],
        out_specs=[
            pl.BlockSpec((bm, bn), lambda j, i: (i, j)),
            pl.BlockSpec((ms, bn), lambda j, i: (0, j)),
        ],
        out_shape=[jax.ShapeDtypeStruct((m, n), F32), jax.ShapeDtypeStruct((ms, n), F32)],
        scratch_shapes=[pltpu.VMEM((d, bn), BF16)],
        compiler_params=pltpu.CompilerParams(
            dimension_semantics=("arbitrary", "arbitrary"),
            vmem_limit_bytes=_vmem_limit(est + (4 << 20))),
        name=f"proj_rot_hd{hd}",
    )(h, hs, w, ctab, stab, *tabs_s)


def _out_kernel(x_ref, oa_ref, or_ref, xs_ref, os_ref, w_ref, y_ref, ys_ref, w_scr):
    i = pl.program_id(1)
    ka = oa_ref.shape[1]

    @pl.when(i == 0)
    def _first():
        w_scr[...] = w_ref[...].astype(BF16)
        ys_ref[...] = xs_ref[...] + _dot(os_ref[...], w_scr[...])

    y_ref[...] = x_ref[...] + (_dot(oa_ref[...], w_scr[:ka, :]) + _dot(or_ref[...], w_scr[ka:, :]))


def _out_proj(x, oa, orr, xs, os_, w, *, bm, bn=512):
    m, d = x.shape
    ms = xs.shape[0]
    ka, kr = oa.shape[1], orr.shape[1]
    assert m % bm == 0 and d % bn == 0 and w.shape[0] == ka + kr and os_.shape == (ms, ka + kr)
    est = 2 * bm * (ka + kr) * 2 + 2 * (ka + kr) * bn * 4 + (ka + kr) * bn * 2 + 6 * bm * bn * 4
    return pl.pallas_call(
        _out_kernel,
        grid=(d // bn, m // bm),
        in_specs=[
            pl.BlockSpec((bm, bn), lambda j, i: (i, j)),
            pl.BlockSpec((bm, ka), lambda j, i: (i, 0)),
            pl.BlockSpec((bm, kr), lambda j, i: (i, 0)),
            pl.BlockSpec((ms, bn), lambda j, i: (0, j)),
            pl.BlockSpec((ms, ka + kr), lambda j, i: (0, 0)),
            pl.BlockSpec((ka + kr, bn), lambda j, i: (0, j)),
        ],
        out_specs=[
            pl.BlockSpec((bm, bn), lambda j, i: (i, j)),
            pl.BlockSpec((ms, bn), lambda j, i: (0, j)),
        ],
        out_shape=[jax.ShapeDtypeStruct((m, d), F32), jax.ShapeDtypeStruct((ms, d), F32)],
        scratch_shapes=[pltpu.VMEM((ka + kr, bn), BF16)],
        compiler_params=pltpu.CompilerParams(
            dimension_semantics=("arbitrary", "arbitrary"),
            vmem_limit_bytes=_vmem_limit(est + (4 << 20))),
        name="out_proj",
    )(x, oa, orr, xs, os_, w)


def _kv_kernel(*refs, n_layers, n_heads, hd):
    k_refs = refs[:n_layers]
    v_refs = refs[n_layers:2 * n_layers]
    ko_ref, vo_ref = refs[2 * n_layers:]
    layer = pl.program_id(0)
    bm = k_refs[0].shape[0]
    for l in range(n_layers):
        @pl.when(layer == l)
        def _(l=l):
            for h in range(n_heads):
                rows = pl.ds(h, bm, stride=n_heads)
                ko_ref[rows, :] = k_refs[l][:, h * hd:(h + 1) * hd]
                vo_ref[rows, :] = v_refs[l][:, h * hd:(h + 1) * hd]


def _kv_outputs(zas, *, n_heads, bm=256):
    n_layers = len(zas)
    m = zas[0].shape[0]
    hd = ATTN_HEAD_DIM
    da = n_heads * hd
    nm = m // bm
    assert m % bm == 0

    def in_spec(l, col):
        return pl.BlockSpec((bm, da), lambda ll, i, l=l: (jnp.where(ll == l, i, jnp.where(ll < l, 0, nm - 1)), col))

    out_spec = pl.BlockSpec((bm * n_heads, hd), lambda ll, i: (ll * nm + i, 0))
    out_sds = jax.ShapeDtypeStruct((n_layers * m * n_heads, hd), F32)
    return pl.pallas_call(
        functools.partial(_kv_kernel, n_layers=n_layers, n_heads=n_heads, hd=hd),
        grid=(n_layers, nm),
        in_specs=[in_spec(l, 1) for l in range(n_layers)] + [in_spec(l, 2) for l in range(n_layers)],
        out_specs=[out_spec, out_spec],
        out_shape=[out_sds, out_sds],
        compiler_params=pltpu.CompilerParams(
            dimension_semantics=("arbitrary", "arbitrary"),
            vmem_limit_bytes=_vmem_limit(2 * 2 * (n_layers + 1) * bm * da * 4 + (8 << 20))),
        name="kv_window_outputs",
    )(*zas, *zas)


def _attn_block(qb, kw, vw, first):
    s = _dot_nt(qb, kw)
    qi = lax.broadcasted_iota(jnp.int32, s.shape, 0)
    kj = lax.broadcasted_iota(jnp.int32, s.shape, 1)
    if first:
        mask = kj <= qi
    else:
        mask = (kj >= qi) & (kj <= qi + DIL_BLOCK)
    s = jnp.where(mask, s, NEG_INF)
    m = jnp.max(s, axis=-1, keepdims=True)
    p = jnp.where(mask, jnp.exp(s - m), 0.0)
    l = jnp.sum(p, axis=-1, keepdims=True)
    u = _dot(p.astype(BF16), vw)
    return u, m, l


def _attn_kernel(q_ref, k_ref, v_ref, o_ref, u_scr, m_scr, l_scr, *, seq):
    blk = DIL_BLOCK
    hd = q_ref.shape[2]

    def load(ref, start, size, stride):
        if stride == 1:
            return ref[0, pl.ds(start, size), :].astype(BF16)
        return ref[0, pl.ds(start, size, stride=stride), :].astype(BF16)

    def block(r, c, n):
        q0 = c + r * n * blk
        qb = load(q_ref, q0, blk, r)
        if n == 0:
            kw = load(k_ref, q0, blk, r)
            vw = load(v_ref, q0, blk, r)
        else:
            k0 = c + r * (n - 1) * blk
            kw = load(k_ref, k0, 2 * blk, r)
            vw = load(v_ref, k0, 2 * blk, r)
        u, m, l = _attn_block(qb, kw, vw, n == 0)
        if r == 1:
            rows = pl.ds(q0, blk)
        else:
            rows = pl.ds(q0, blk, stride=r)
        return rows, u, m, l

    first_r = DILATIONS[0][1]
    last_r = DILATIONS[-1][1]
    for (_, r) in DILATIONS:
        lsub = seq // r
        nb = -(-lsub // blk)
        for c in range(r):
            for n in range(nb):
                rows, u, m, l = block(r, c, n)
                m = jnp.broadcast_to(m, (blk, hd))
                l = jnp.broadcast_to(l, (blk, hd))
                if r == first_r:
                    m_run, l_run, u_run = m, l, u
                else:
                    m_old = m_scr[rows, :]
                    m_run = jnp.maximum(m_old, m)
                    a_old = jnp.exp(m_old - m_run)
                    a_new = jnp.exp(m - m_run)
                    l_run = a_old * l_scr[rows, :] + a_new * l
                    u_run = a_old * u_scr[rows, :] + a_new * u
                if r == last_r:
                    u_scr[rows, :] = u_run / l_run
                else:
                    u_scr[rows, :] = u_run
                    m_scr[rows, :] = m_run
                    l_scr[rows, :] = l_run
    o_ref[0] = u_scr[...].astype(BF16)


def _attention(za, *, batch, seq, n_heads):
    hd = ATTN_HEAD_DIM
    z3 = za.reshape(batch, seq, 3 * n_heads * hd)
    est = 3 * 2 * seq * hd * 4 + 2 * seq * hd * 2 + 3 * seq * hd * 4
    o = pl.pallas_call(
        functools.partial(_attn_kernel, seq=seq),
        grid=(batch, n_heads),
        in_specs=[
            pl.BlockSpec((1, seq, hd), lambda b, h: (b, 0, h)),
            pl.BlockSpec((1, seq, hd), lambda b, h: (b, 0, n_heads + h)),
            pl.BlockSpec((1, seq, hd), lambda b, h: (b, 0, 2 * n_heads + h)),
        ],
        out_specs=pl.BlockSpec((1, seq, hd), lambda b, h: (b, 0, h)),
        out_shape=jax.ShapeDtypeStruct((batch, seq, n_heads * hd), BF16),
        scratch_shapes=[pltpu.VMEM((seq, hd), F32)] * 3,
        compiler_params=pltpu.CompilerParams(
            dimension_semantics=("arbitrary", "arbitrary"),
            vmem_limit_bytes=_vmem_limit(est + (32 << 20))),
        name="dilated_attn_prompt",
    )(z3, z3, z3)
    return o.reshape(batch * seq, n_heads * hd)


RET_HEADS_PER_STEP = 2


def _ret_kernel(gpow_ref, q_ref, k_ref, v_ref, g_ref, dec_ref, qd_ref, kd_ref, o_ref, st_ref, *, seq, hps):
    ck = RET_CHUNK
    hd = RET_HEAD_DIM
    h0 = pl.program_id(1) * hps
    st_ref[...] = jnp.zeros_like(st_ref)

    def chunk(t, carry):
        rows = pl.ds(pl.multiple_of(t * ck, ck), ck)
        for hh in range(hps):
            cols = slice(hh * hd, (hh + 1) * hd)
            q = q_ref[0, rows, cols].astype(BF16)
            k = k_ref[0, rows, cols]
            v = v_ref[0, rows, cols].astype(BF16)
            inner = _dot_nt(q, k.astype(BF16)) * dec_ref[hh]
            intra = _dot(inner.astype(BF16), v)
            st = st_ref[0, hh]
            cross = _dot(q, st.astype(BF16)) * qd_ref[hh]
            o = intra + cross
            kdec = (k * kd_ref[hh]).astype(BF16)
            st_ref[0, hh] = gpow_ref[h0 + hh] * st + _dot_tn(kdec, v)
            o = o * lax.rsqrt(jnp.mean(o * o, axis=-1, keepdims=True) + RMS_EPS)
            g = g_ref[0, rows, cols]
            o_ref[0, rows, cols] = (o * (g * jax.nn.sigmoid(g))).astype(BF16)
        return carry

    lax.fori_loop(0, seq // ck, chunk, 0)


def _ret_tables(n_heads, hd):
    hh = jnp.arange(n_heads, dtype=F32)
    ld = jnp.log1p(-jnp.exp2(-5.0 - hh))
    n = jnp.arange(RET_CHUNK, dtype=F32)
    diff = n[:, None] - n[None, :]
    dec = jnp.where(diff >= 0, jnp.exp(jnp.maximum(diff, 0.0)[None] * ld[:, None, None]), 0.0)
    qd = jnp.exp((n[None, :] + 1.0) * ld[:, None])
    kd = jnp.exp((RET_CHUNK - 1.0 - n)[None, :] * ld[:, None])
    qd = jnp.broadcast_to(qd[:, :, None], (n_heads, RET_CHUNK, hd))
    kd = jnp.broadcast_to(kd[:, :, None], (n_heads, RET_CHUNK, hd))
    gpow = jnp.exp(RET_CHUNK * ld)
    return ld, dec, qd, kd, gpow


def _retention(zr, *, batch, seq, n_heads):
    hd = RET_HEAD_DIM
    hps = RET_HEADS_PER_STEP if n_heads % RET_HEADS_PER_STEP == 0 else 1
    ng = n_heads // hps
    z3 = zr.reshape(batch, seq, 4 * n_heads * hd)
    _, dec, qd, kd, gpow = _ret_tables(n_heads, hd)
    blk = lambda off: pl.BlockSpec((1, seq, hps * hd), lambda b, h: (b, 0, off + h))
    tab = lambda r, c: pl.BlockSpec((hps, r, c), lambda b, h: (h, 0, 0))
    est = 4 * 2 * seq * hps * hd * 4 + 2 * seq * hps * hd * 2 + 2 * hps * hd * hd * 4
    o, st = pl.pallas_call(
        functools.partial(_ret_kernel, seq=seq, hps=hps),
        grid=(batch, ng),
        in_specs=[
            pl.BlockSpec(memory_space=pltpu.SMEM),
            blk(0), blk(ng), blk(2 * ng), blk(3 * ng),
            tab(RET_CHUNK, RET_CHUNK), tab(RET_CHUNK, hd), tab(RET_CHUNK, hd),
        ],
        out_specs=[
            pl.BlockSpec((1, seq, hps * hd), lambda b, h: (b, 0, h)),
            pl.BlockSpec((1, hps, hd, hd), lambda b, h: (b, h, 0, 0)),
        ],
        out_shape=[
            jax.ShapeDtypeStruct((batch, seq, n_heads * hd), BF16),
            jax.ShapeDtypeStruct((batch, n_heads, hd, hd), F32),
        ],
        compiler_params=pltpu.CompilerParams(
            dimension_semantics=("arbitrary", "arbitrary"),
            vmem_limit_bytes=_vmem_limit(est + (16 << 20))),
        name="retention_prompt",
    )(gpow, z3, z3, z3, z3, dec, qd, kd)
    return o.reshape(batch * seq, n_heads * hd), st


def _bf(x):
    return x.astype(BF16).astype(F32)


def _sample_kernel(gam_ref, qkv_ref, zr_ref, k1_ref, k4_ref, k16_ref, v1_ref, v4_ref, v16_ref, st_ref,
                   oa_ref, or_ref, ns_ref, *, n_rh):
    rhd = RET_HEAD_DIM
    dr = n_rh * rhd
    qb = _bf(qkv_ref[0, 0])
    vnb = _bf(qkv_ref[0, 2])
    s_new = jnp.sum(qb * _bf(qkv_ref[0, 1]), axis=-1, keepdims=True)
    m_run = l_run = u_run = None
    for kb, vb in ((k1_ref, v1_ref), (k4_ref, v4_ref), (k16_ref, v16_ref)):
        s = jnp.sum(_bf(kb[0, :, 0]) * qb[None], axis=-1, keepdims=True)
        m = jnp.maximum(jnp.max(s, axis=0), s_new)
        p = jnp.exp(s - m[None])
        pn = jnp.exp(s_new - m)
        l = jnp.sum(p, axis=0) + pn
        u = jnp.sum(_bf(p) * _bf(vb[0, :, 0]), axis=0) + pn * vnb
        if m_run is None:
            m_run, l_run, u_run = m, l, u
        else:
            m_new = jnp.maximum(m_run, m)
            a_old = jnp.exp(m_run - m_new)
            a_new = jnp.exp(m - m_new)
            l_run = a_old * l_run + a_new * l
            u_run = a_old * u_run + a_new * u
            m_run = m_new
    oa_ref[0] = u_run / l_run

    zr = zr_ref[0]
    row0 = lax.broadcasted_iota(jnp.int32, (8, rhd), 0) == 0
    for h in range(n_rh):
        gam = gam_ref[h]
        q = zr[:, h * rhd:(h + 1) * rhd]
        k = zr[:, dr + h * rhd: dr + (h + 1) * rhd]
        v = zr[:, 2 * dr + h * rhd: 2 * dr + (h + 1) * rhd]
        g = zr[:, 3 * dr + h * rhd: 3 * dr + (h + 1) * rhd]
        st = st_ref[0, h]
        q8 = jnp.broadcast_to(q, (8, rhd)).astype(BF16)
        cross = _dot(q8, st.astype(BF16))[:1] * gam
        inner = _bf(jnp.sum(_bf(q) * _bf(k), axis=-1, keepdims=True))
        o = inner * _bf(v) + cross
        k8 = jnp.where(row0, jnp.broadcast_to(k, (8, rhd)), 0.0).astype(BF16)
        v8 = jnp.broadcast_to(v, (8, rhd)).astype(BF16)
        ns_ref[0, h] = gam * st + _dot_tn(k8, v8)
        o = o * lax.rsqrt(jnp.mean(o * o, axis=-1, keepdims=True) + RMS_EPS)
        or_ref[0, :, h * rhd:(h + 1) * rhd] = o * (g * jax.nn.sigmoid(g))


def _sample_mixer(za, zr, cache_k, cache_v, state, layer, *, n_ah, n_rh):
    nl, nb_, n_buf = cache_k.shape[:3]
    ahd, rhd = ATTN_HEAD_DIM, RET_HEAD_DIM
    da = n_ah * ahd
    dr = n_rh * rhd
    assert n_buf == DILATIONS[-1][0]
    span = DIL_BLOCK
    views, specs = [], []
    for cache in (cache_k, cache_v):
        for (_, r) in DILATIONS:
            views.append(cache.reshape(nl * nb_, n_buf // r, r, n_ah, ahd))
            rb = (n_buf // r) // span - 1
            specs.append(pl.BlockSpec((1, span, 1, n_ah, ahd), lambda b, rb=rb: (layer * nb_ + b, rb, 0, 0, 0)))
    ld = jnp.log1p(-jnp.exp2(-5.0 - jnp.arange(n_rh, dtype=F32)))
    gam = jnp.exp(ld)
    st4 = state.reshape(nl * nb_, n_rh, rhd, rhd)
    est = 2 * 6 * span * da * 4 + 4 * n_rh * rhd * rhd * 4 + 8 * span * n_ah * 128 * 4
    oa, orr, ns = pl.pallas_call(
        functools.partial(_sample_kernel, n_rh=n_rh),
        grid=(nb_,),
        in_specs=[
            pl.BlockSpec(memory_space=pltpu.SMEM),
            pl.BlockSpec((1, 3, n_ah, ahd), lambda b: (b, 0, 0, 0)),
            pl.BlockSpec((1, 1, 4 * dr), lambda b: (b, 0, 0)),
            *specs,
            pl.BlockSpec((1, n_rh, rhd, rhd), lambda b: (layer * nb_ + b, 0, 0, 0)),
        ],
        out_specs=[
            pl.BlockSpec((1, n_ah, ahd), lambda b: (b, 0, 0)),
            pl.BlockSpec((1, 1, dr), lambda b: (b, 0, 0)),
            pl.BlockSpec((1, n_rh, rhd, rhd), lambda b: (b, 0, 0, 0)),
        ],
        out_shape=[
            jax.ShapeDtypeStruct((nb_, n_ah, ahd), F32),
            jax.ShapeDtypeStruct((nb_, 1, dr), F32),
            jax.ShapeDtypeStruct((nb_, n_rh, rhd, rhd), F32),
        ],
        compiler_params=pltpu.CompilerParams(
            dimension_semantics=("arbitrary",),
            vmem_limit_bytes=_vmem_limit(est + (16 << 20))),
        name="sample_mixer",
    )(gam, za.reshape(nb_, 3, n_ah, ahd), zr.reshape(nb_, 1, 4 * dr), *views, st4)
    return jnp.concatenate([oa.reshape(nb_, da), orr.reshape(nb_, dr)], axis=-1), ns


def _rot_tables(pos, inv_freq, scales):
    ang = pos.astype(F32)[:, None] * inv_freq[None, :]
    cos, sin = jnp.cos(ang), jnp.sin(ang)
    c_full = jnp.concatenate([cos, cos], axis=-1)
    s_full = jnp.concatenate([-sin, sin], axis=-1)
    cs, ss = [], []
    for sc in scales:
        if sc is None:
            cs.append(jnp.ones_like(c_full))
            ss.append(jnp.zeros_like(s_full))
        else:
            cs.append(c_full * sc)
            ss.append(s_full * sc)
    return jnp.stack(cs), jnp.stack(ss)


def _tables(pos):
    inv_a = ROPE_THETA ** (-jnp.arange(0, ATTN_HEAD_DIM, 2, dtype=F32) / ATTN_HEAD_DIM)
    inv_r = 1.0 / (10000.0 ** jnp.linspace(0.0, 1.0, RET_HEAD_DIM // 2, dtype=F32))
    ta = _rot_tables(pos, inv_a, (ATTN_HEAD_DIM ** -0.5, 1.0, None))
    tr = _rot_tables(pos, inv_r, (1.0, RET_HEAD_DIM ** -0.5, None, None))
    return ta, tr


def kernel(x_prompt, x_sample, cache_attn_k, cache_attn_v, state_ret, g_ffn1, w_ffn1_gate, w_ffn1_up,
           w_ffn1_down, g_mix, w_in, w_out, g_ffn2, w_ffn2_gate, w_ffn2_up, w_ffn2_down, g_final):
    batch, seq, d = x_prompt.shape
    dec_batch, t_new, _ = x_sample.shape
    assert t_new == 1 and dec_batch <= SAMPLE_ROWS
    depth = g_ffn1.shape[0]
    d_attn = d // 2
    d_ret = d // 2
    n_ah = d_attn // ATTN_HEAD_DIM
    n_rh = d_ret // RET_HEAD_DIM
    m = batch * seq
    bm = min(1024, seq)
    ms = SAMPLE_ROWS
    assert seq == DILATIONS[-1][0]

    tabs_a, tabs_r = _tables(jnp.arange(seq))
    tabs_as, tabs_rs = _tables(jnp.full((ms,), PAST_LEN))
    proj_a = dict(col0=0, n=3 * d_attn, hd=ATTN_HEAD_DIM, seg=d_attn, bm=bm)
    proj_r = dict(col0=3 * d_attn, n=4 * d_ret, hd=RET_HEAD_DIM, seg=d_ret, bm=bm)

    xp = x_prompt.reshape(m, d)
    xs = jnp.pad(x_sample.reshape(dec_batch, d), ((0, ms - dec_batch), (0, 0)))
    zas_p, ps, sk, sv, ss = [], [], [], [], []
    for l in range(depth):
        xp, hp, xs, hs = _ffn(xp, xs, g_ffn1[l], w_ffn1_gate[l], w_ffn1_up[l], w_ffn1_down[l], g_mix[l],
                              mode="norm", bm=bm)
        za, za_s = _proj(hp, hs, w_in[l], tabs_a, tabs_as, **proj_a)
        zr, zr_s = _proj(hp, hs, w_in[l], tabs_r, tabs_rs, **proj_r)
        oa = _attention(za, batch=batch, seq=seq, n_heads=n_ah)
        orr, st_p = _retention(zr, batch=batch, seq=seq, n_heads=n_rh)
        o_s, st_s = _sample_mixer(za_s[:dec_batch], zr_s[:dec_batch], cache_attn_k, cache_attn_v, state_ret, l,
                                  n_ah=n_ah, n_rh=n_rh)
        o_s = jnp.pad(o_s, ((0, ms - dec_batch), (0, 0))).astype(BF16)
        xp, xs = _out_proj(xp, oa, orr, xs, o_s, w_out[l], bm=bm)
        last = l == depth - 1
        xp, xs = _ffn(xp, xs, g_ffn2[l], w_ffn2_gate[l], w_ffn2_up[l], w_ffn2_down[l], g_final,
                      mode="final" if last else "mid", bm=bm)
        zas_p.append(za)
        ps.append(st_p)
        za_s4 = za_s[:dec_batch].reshape(dec_batch, 1, 3, n_ah, ATTN_HEAD_DIM)
        sk.append(za_s4[:, :, 1])
        sv.append(za_s4[:, :, 2])
        ss.append(st_s)

    pk, pv = _kv_outputs(zas_p, n_heads=n_ah)
    kv_shape = (depth, batch, seq, n_ah, ATTN_HEAD_DIM)
    y_prompt = xp.reshape(batch, seq, d)
    y_sample = xs[:dec_batch].reshape(dec_batch, t_new, d)
    return (y_prompt, y_sample, pk.reshape(kv_shape), pv.reshape(kv_shape), jnp.stack(ps),
            jnp.stack(sk), jnp.stack(sv), jnp.stack(ss))
```

```python
import functools

import jax
import jax.numpy as jnp
from jax import lax
from jax.experimental import pallas as pl
from jax.experimental.pallas import tpu as pltpu

F32 = jnp.float32
BF16 = jnp.bfloat16

ATTN_HEAD_DIM = 128
RET_HEAD_DIM = 256
DILATIONS = ((128, 1), (512, 4), (2048, 16))
DIL_BLOCK = 128
RET_CHUNK = 128
ROPE_THETA = 10000.0
RMS_EPS = 1e-6
NEG_INF = -1e30
PAST_LEN = 8192

V7X_VMEM_BYTES = 64 * 1024 * 1024
VMEM_HEADROOM_BYTES = 3 * 1024 * 1024
SAMPLE_ROWS = 16


def _vmem_limit(estimate_bytes):
    return int(min(V7X_VMEM_BYTES - VMEM_HEADROOM_BYTES, max(estimate_bytes, 16 * 1024 * 1024)))


def _rms(x, g):
    return x * lax.rsqrt(jnp.mean(x * x, axis=-1, keepdims=True) + RMS_EPS) * g


def _dot(a, b):
    return jnp.dot(a, b, preferred_element_type=F32)


def _dot_nt(a, b):
    return lax.dot_general(a, b, (((1,), (1,)), ((), ())), preferred_element_type=F32)


def _dot_tn(a, b):
    return lax.dot_general(a, b, (((0,), (0,)), ((), ())), preferred_element_type=F32)


def _ffn_kernel(x_hbm, xs_ref, g_ref, wg_ref, wu_ref, wd_ref, gn_ref, *rest, bm, mode, row_chunk, col_chunk):
    if mode == "norm":
        y_hbm, hn_hbm, ys_ref, hns_ref, acc, h_scr, hs_scr, sem = rest
    else:
        y_hbm, ys_ref, acc, h_scr, hs_scr, sem = rest
    i = pl.program_id(0)
    j = pl.program_id(1)
    nj = pl.num_programs(1)
    rows = pl.ds(pl.multiple_of(i * bm, bm), bm)
    d = acc.shape[1]
    n_row_chunks = bm // row_chunk

    def for_row_chunks(fn):
        def body(t, carry):
            fn(pl.ds(pl.multiple_of(t * row_chunk, row_chunk), row_chunk))
            return carry
        lax.fori_loop(0, n_row_chunks, body, 0)

    @pl.when(j == 0)
    def _load():
        cp = pltpu.make_async_copy(x_hbm.at[rows], acc, sem.at[0])
        cp.start()
        cp.wait()

        def norm_rows(r):
            h_scr[r, :] = _rms(acc[r, :], g_ref[...]).astype(BF16)
        for_row_chunks(norm_rows)

    wg = wg_ref[...].astype(BF16)
    wu = wu_ref[...].astype(BF16)
    wd = wd_ref[...].astype(BF16)

    def half_step(h):
        gate = _dot(h, wg)
        up = _dot(h, wu)
        return ((0.5 * gate) * jax.nn.sigmoid(gate) * up).astype(BF16)

    a = half_step(h_scr[...])
    for c0 in range(0, d, col_chunk):
        acc[:, c0:c0 + col_chunk] += _dot(a, wd[:, c0:c0 + col_chunk])

    @pl.when(i == 0)
    def _sample_rows():
        @pl.when(j == 0)
        def _():
            xs = xs_ref[...]
            ys_ref[...] = xs
            hs_scr[...] = _rms(xs, g_ref[...]).astype(BF16)

        ys_ref[...] += _dot(half_step(hs_scr[...]), wd)

        @pl.when(j == nj - 1)
        def _():
            if mode == "final":
                ys_ref[...] = _rms(ys_ref[...], gn_ref[...])
            elif mode == "norm":
                hns_ref[...] = _rms(ys_ref[...], gn_ref[...]).astype(BF16)

    @pl.when(j == nj - 1)
    def _store():
        if mode == "final":
            def fin_rows(r):
                acc[r, :] = _rms(acc[r, :], gn_ref[...])
            for_row_chunks(fin_rows)
            cp = pltpu.make_async_copy(acc, y_hbm.at[rows], sem.at[0])
            cp.start()
            cp.wait()
        else:
            cp = pltpu.make_async_copy(acc, y_hbm.at[rows], sem.at[0])
            cp.start()
            if mode == "norm":
                def nrm_rows(r):
                    h_scr[r, :] = _rms(acc[r, :], gn_ref[...]).astype(BF16)
                for_row_chunks(nrm_rows)
                cp2 = pltpu.make_async_copy(h_scr, hn_hbm.at[rows], sem.at[1])
                cp2.start()
                cp2.wait()
            cp.wait()


def _ffn(x, xs, g, wg, wu, wd, gn, *, layer, mode, bm, bn=256):
    m, d = x.shape
    ms = xs.shape[0]
    dff = wg.shape[2]
    assert m % bm == 0 and dff % bn == 0
    row_chunk = min(bm, 64)
    col_chunk = min(d, 512)
    any_spec = pl.BlockSpec(memory_space=pl.ANY)
    small = pl.BlockSpec((ms, d), lambda i, j: (0, 0))
    vec = pl.BlockSpec((1, d), lambda i, j: (0, 0))
    out_shape = [jax.ShapeDtypeStruct((m, d), F32)]
    out_specs = [any_spec]
    if mode == "norm":
        out_shape.append(jax.ShapeDtypeStruct((m, d), BF16))
        out_specs.append(any_spec)
    out_shape.append(jax.ShapeDtypeStruct((ms, d), F32))
    out_specs.append(small)
    if mode == "norm":
        out_shape.append(jax.ShapeDtypeStruct((ms, d), BF16))
        out_specs.append(small)
    est = bm * d * 6 + 3 * 2 * d * bn * 4 + 3 * d * bn * 2 + 4 * bm * bn * 4 + bm * col_chunk * 4
    return pl.pallas_call(
        functools.partial(_ffn_kernel, bm=bm, mode=mode, row_chunk=row_chunk, col_chunk=col_chunk),
        grid=(m // bm, dff // bn),
        in_specs=[
            any_spec,
            small,
            vec,
            pl.BlockSpec((None, d, bn), lambda i, j: (layer, 0, j)),
            pl.BlockSpec((None, d, bn), lambda i, j: (layer, 0, j)),
            pl.BlockSpec((None, bn, d), lambda i, j: (layer, j, 0)),
            vec,
        ],
        out_specs=out_specs,
        out_shape=out_shape,
        scratch_shapes=[
            pltpu.VMEM((bm, d), F32),
            pltpu.VMEM((bm, d), BF16),
            pltpu.VMEM((ms, d), BF16),
            pltpu.SemaphoreType.DMA((2,)),
        ],
        compiler_params=pltpu.CompilerParams(
            dimension_semantics=("arbitrary", "arbitrary"),
            vmem_limit_bytes=_vmem_limit(est + (4 << 20))),
        name=f"ffn_{mode}",
    )(x, xs, g.reshape(1, d), wg, wu, wd, gn.reshape(1, d))


def _proj_kernel(h_ref, hs_ref, w_ref, c_ref, s_ref, cs_ref, ss_ref, o_ref, os_ref, w_scr, *, hd):
    i = pl.program_id(1)

    def rotate_store(z, cc, ss, out):
        for c0 in range(0, z.shape[1], hd):
            zh = z[:, c0:c0 + hd]
            out[:, c0:c0 + hd] = zh * cc + pltpu.roll(zh, hd // 2, axis=1) * ss

    @pl.when(i == 0)
    def _first():
        w_scr[...] = w_ref[...].astype(BF16)
        rotate_store(_dot(hs_ref[...], w_scr[...]), cs_ref[0], ss_ref[0], os_ref)

    rotate_store(_dot(h_ref[...], w_scr[...]), c_ref[0], s_ref[0], o_ref)


def _proj(h, hs, w, tabs, tabs_s, *, layer, col0, n, hd, seg, bm):
    m, d = h.shape
    ms = hs.shape[0]
    ctab, stab = tabs
    p = ctab.shape[1]
    bn = 1024 if seg % 1024 == 0 else 512
    assert m % bm == 0 and n % bn == 0 and col0 % bn == 0 and seg % bn == 0 and p % bm == 0
    npb = p // bm
    cb0 = col0 // bn
    spc = seg // bn
    est = 2 * bm * d * 2 + d * bn * 4 + d * bn * 2 + 4 * bm * bn * 4 + 4 * bm * hd * 4
    tab = pl.BlockSpec((1, bm, hd), lambda j, i: (j // spc, i % npb, 0))
    tab_s = pl.BlockSpec((1, ms, hd), lambda j, i: (j // spc, 0, 0))
    return pl.pallas_call(
        functools.partial(_proj_kernel, hd=hd),
        grid=(n // bn, m // bm),
        in_specs=[
            pl.BlockSpec((bm, d), lambda j, i: (i, 0)),
            pl.BlockSpec((ms, d), lambda j, i: (0, 0)),
            pl.BlockSpec((None, d, bn), lambda j, i: (layer, 0, cb0 + j), pipeline_mode=pl.Buffered(1)),
            tab, tab, tab_s, tab_s,
        ],
        out_specs=[
            pl.BlockSpec((bm, bn), lambda j, i: (i, j)),
            pl.BlockSpec((ms, bn), lambda j, i: (0, j)),
        ],
        out_shape=[jax.ShapeDtypeStruct((m, n), F32), jax.ShapeDtypeStruct((ms, n), F32)],
        scratch_shapes=[pltpu.VMEM((d, bn), BF16)],
        compiler_params=pltpu.CompilerParams(
            dimension_semantics=("arbitrary", "arbitrary"),
            vmem_limit_bytes=_vmem_limit(est + (4 << 20))),
        name=f"proj_rot_hd{hd}",
    )(h, hs, w, ctab, stab, *tabs_s)


def _out_kernel(x_ref, oa_ref, or_ref, xs_ref, os_ref, w_ref, y_ref, ys_ref, w_scr):
    i = pl.program_id(1)
    ka = oa_ref.shape[1]

    @pl.when(i == 0)
    def _first():
        w_scr[...] = w_ref[...].astype(BF16)
        ys_ref[...] = xs_ref[...] + _dot(os_ref[...], w_scr[...])

    y_ref[...] = x_ref[...] + (_dot(oa_ref[...], w_scr[:ka, :]) + _dot(or_ref[...], w_scr[ka:, :]))


def _out_proj(x, oa, orr, xs, os_, w, *, layer, bm):
    m, d = x.shape
    ms = xs.shape[0]
    ka, kr = oa.shape[1], orr.shape[1]
    bn = 1024 if d % 1024 == 0 else 512
    assert m % bm == 0 and d % bn == 0 and w.shape[1] == ka + kr and os_.shape == (ms, ka + kr)
    est = 2 * bm * (ka + kr) * 2 + (ka + kr) * bn * 4 + (ka + kr) * bn * 2 + 6 * bm * bn * 4
    return pl.pallas_call(
        _out_kernel,
        grid=(d // bn, m // bm),
        in_specs=[
            pl.BlockSpec((bm, bn), lambda j, i: (i, j)),
            pl.BlockSpec((bm, ka), lambda j, i: (i, 0)),
            pl.BlockSpec((bm, kr), lambda j, i: (i, 0)),
            pl.BlockSpec((ms, bn), lambda j, i: (0, j)),
            pl.BlockSpec((ms, ka + kr), lambda j, i: (0, 0)),
            pl.BlockSpec((None, ka + kr, bn), lambda j, i: (layer, 0, j), pipeline_mode=pl.Buffered(1)),
        ],
        out_specs=[
            pl.BlockSpec((bm, bn), lambda j, i: (i, j)),
            pl.BlockSpec((ms, bn), lambda j, i: (0, j)),
        ],
        out_shape=[jax.ShapeDtypeStruct((m, d), F32), jax.ShapeDtypeStruct((ms, d), F32)],
        scratch_shapes=[pltpu.VMEM((ka + kr, bn), BF16)],
        compiler_params=pltpu.CompilerParams(
            dimension_semantics=("arbitrary", "arbitrary"),
            vmem_limit_bytes=_vmem_limit(est + (4 << 20))),
        name="out_proj",
    )(x, oa, orr, xs, os_, w)


def _kv_kernel(*refs, n_layers, n_heads, hd):
    k_refs = refs[:n_layers]
    v_refs = refs[n_layers:2 * n_layers]
    ko_ref, vo_ref = refs[2 * n_layers:]
    layer = pl.program_id(0)
    bm = k_refs[0].shape[0]
    for l in range(n_layers):
        @pl.when(layer == l)
        def _(l=l):
            for h in range(n_heads):
                rows = pl.ds(h, bm, stride=n_heads)
                ko_ref[rows, :] = k_refs[l][:, h * hd:(h + 1) * hd]
                vo_ref[rows, :] = v_refs[l][:, h * hd:(h + 1) * hd]


def _kv_outputs(zas, *, n_heads, bm=256):
    n_layers = len(zas)
    m = zas[0].shape[0]
    hd = ATTN_HEAD_DIM
    da = n_heads * hd
    nm = m // bm
    assert m % bm == 0

    def in_spec(l, col):
        return pl.BlockSpec((bm, da), lambda ll, i, l=l: (jnp.where(ll == l, i, jnp.where(ll < l, 0, nm - 1)), col))

    out_spec = pl.BlockSpec((bm * n_heads, hd), lambda ll, i: (ll * nm + i, 0))
    out_sds = jax.ShapeDtypeStruct((n_layers * m * n_heads, hd), F32)
    return pl.pallas_call(
        functools.partial(_kv_kernel, n_layers=n_layers, n_heads=n_heads, hd=hd),
        grid=(n_layers, nm),
        in_specs=[in_spec(l, 1) for l in range(n_layers)] + [in_spec(l, 2) for l in range(n_layers)],
        out_specs=[out_spec, out_spec],
        out_shape=[out_sds, out_sds],
        compiler_params=pltpu.CompilerParams(
            dimension_semantics=("arbitrary", "arbitrary"),
            vmem_limit_bytes=_vmem_limit(2 * 2 * (n_layers + 1) * bm * da * 4 + (8 << 20))),
        name="kv_window_outputs",
    )(*zas, *zas)


def _attn_block(qb, kw, vw, first):
    s = _dot_nt(qb, kw)
    qi = lax.broadcasted_iota(jnp.int32, s.shape, 0)
    kj = lax.broadcasted_iota(jnp.int32, s.shape, 1)
    if first:
        mask = kj <= qi
    else:
        mask = (kj >= qi) & (kj <= qi + DIL_BLOCK)
    s = jnp.where(mask, s, NEG_INF)
    m = jnp.max(s, axis=-1, keepdims=True)
    p = jnp.where(mask, jnp.exp(s - m), 0.0)
    l = jnp.sum(p, axis=-1, keepdims=True)
    u = _dot(p.astype(BF16), vw)
    return u, m, l


def _attn_kernel(q_ref, k_ref, v_ref, o_ref, u_scr, m_scr, l_scr, *, seq):
    blk = DIL_BLOCK
    hd = q_ref.shape[2]

    def load(ref, start, size, stride):
        if stride == 1:
            return ref[0, pl.ds(start, size), :].astype(BF16)
        return ref[0, pl.ds(start, size, stride=stride), :].astype(BF16)

    def block(r, c, n):
        q0 = c + r * n * blk
        qb = load(q_ref, q0, blk, r)
        if n == 0:
            kw = load(k_ref, q0, blk, r)
            vw = load(v_ref, q0, blk, r)
        else:
            k0 = c + r * (n - 1) * blk
            kw = load(k_ref, k0, 2 * blk, r)
            vw = load(v_ref, k0, 2 * blk, r)
        u, m, l = _attn_block(qb, kw, vw, n == 0)
        if r == 1:
            rows = pl.ds(q0, blk)
        else:
            rows = pl.ds(q0, blk, stride=r)
        return rows, u, m, l

    first_r = DILATIONS[0][1]
    last_r = DILATIONS[-1][1]
    for (_, r) in DILATIONS:
        lsub = seq // r
        nb = -(-lsub // blk)
        for c in range(r):
            for n in range(nb):
                rows, u, m, l = block(r, c, n)
                m = jnp.broadcast_to(m, (blk, hd))
                l = jnp.broadcast_to(l, (blk, hd))
                if r == first_r:
                    m_run, l_run, u_run = m, l, u
                else:
                    m_old = m_scr[rows, :]
                    m_run = jnp.maximum(m_old, m)
                    a_old = jnp.exp(m_old - m_run)
                    a_new = jnp.exp(m - m_run)
                    l_run = a_old * l_scr[rows, :] + a_new * l
                    u_run = a_old * u_scr[rows, :] + a_new * u
                if r == last_r:
                    u_scr[rows, :] = u_run / l_run
                else:
                    u_scr[rows, :] = u_run
                    m_scr[rows, :] = m_run
                    l_scr[rows, :] = l_run
    o_ref[0] = u_scr[...].astype(BF16)


def _attention(za, *, batch, seq, n_heads):
    hd = ATTN_HEAD_DIM
    z3 = za.reshape(batch, seq, 3 * n_heads * hd)
    est = 3 * 2 * seq * hd * 4 + 2 * seq * hd * 2 + 3 * seq * hd * 4
    o = pl.pallas_call(
        functools.partial(_attn_kernel, seq=seq),
        grid=(batch, n_heads),
        in_specs=[
            pl.BlockSpec((1, seq, hd), lambda b, h: (b, 0, h)),
            pl.BlockSpec((1, seq, hd), lambda b, h: (b, 0, n_heads + h)),
            pl.BlockSpec((1, seq, hd), lambda b, h: (b, 0, 2 * n_heads + h)),
        ],
        out_specs=pl.BlockSpec((1, seq, hd), lambda b, h: (b, 0, h)),
        out_shape=jax.ShapeDtypeStruct((batch, seq, n_heads * hd), BF16),
        scratch_shapes=[pltpu.VMEM((seq, hd), F32)] * 3,
        compiler_params=pltpu.CompilerParams(
            dimension_semantics=("arbitrary", "arbitrary"),
            vmem_limit_bytes=_vmem_limit(est + (32 << 20))),
        name="dilated_attn_prompt",
    )(z3, z3, z3)
    return o.reshape(batch * seq, n_heads * hd)


RET_HEADS_PER_STEP = 2


def _ret_kernel(gpow_ref, q_ref, k_ref, v_ref, g_ref, dec_ref, qd_ref, kd_ref, o_ref, st_ref, *, seq, hps):
    ck = RET_CHUNK
    hd = RET_HEAD_DIM
    h0 = pl.program_id(1) * hps
    st_ref[...] = jnp.zeros_like(st_ref)

    def chunk(t, carry):
        rows = pl.ds(pl.multiple_of(t * ck, ck), ck)
        for hh in range(hps):
            cols = slice(hh * hd, (hh + 1) * hd)
            q = q_ref[0, rows, cols].astype(BF16)
            k = k_ref[0, rows, cols]
            v = v_ref[0, rows, cols].astype(BF16)
            inner = _dot_nt(q, k.astype(BF16)) * dec_ref[hh]
            intra = _dot(inner.astype(BF16), v)
            st = st_ref[0, hh]
            cross = _dot(q, st.astype(BF16)) * qd_ref[hh]
            o = intra + cross
            kdec = (k * kd_ref[hh]).astype(BF16)
            st_ref[0, hh] = gpow_ref[h0 + hh] * st + _dot_tn(kdec, v)
            o = o * lax.rsqrt(jnp.mean(o * o, axis=-1, keepdims=True) + RMS_EPS)
            g = g_ref[0, rows, cols]
            o_ref[0, rows, cols] = (o * (g * jax.nn.sigmoid(g))).astype(BF16)
        return carry

    lax.fori_loop(0, seq // ck, chunk, 0)


def _ret_tables(n_heads, hd):
    hh = jnp.arange(n_heads, dtype=F32)
    ld = jnp.log1p(-jnp.exp2(-5.0 - hh))
    n = jnp.arange(RET_CHUNK, dtype=F32)
    diff = n[:, None] - n[None, :]
    dec = jnp.where(diff >= 0, jnp.exp(jnp.maximum(diff, 0.0)[None] * ld[:, None, None]), 0.0)
    qd = jnp.exp((n[None, :] + 1.0) * ld[:, None])
    kd = jnp.exp((RET_CHUNK - 1.0 - n)[None, :] * ld[:, None])
    qd = jnp.broadcast_to(qd[:, :, None], (n_heads, RET_CHUNK, hd))
    kd = jnp.broadcast_to(kd[:, :, None], (n_heads, RET_CHUNK, hd))
    gpow = jnp.exp(RET_CHUNK * ld)
    return ld, dec, qd, kd, gpow


def _retention(zr, *, batch, seq, n_heads):
    hd = RET_HEAD_DIM
    hps = RET_HEADS_PER_STEP if n_heads % RET_HEADS_PER_STEP == 0 else 1
    ng = n_heads // hps
    z3 = zr.reshape(batch, seq, 4 * n_heads * hd)
    _, dec, qd, kd, gpow = _ret_tables(n_heads, hd)
    blk = lambda off: pl.BlockSpec((1, seq, hps * hd), lambda b, h: (b, 0, off + h))
    tab = lambda r, c: pl.BlockSpec((hps, r, c), lambda b, h: (h, 0, 0))
    est = 4 * 2 * seq * hps * hd * 4 + 2 * seq * hps * hd * 2 + 2 * hps * hd * hd * 4
    o, st = pl.pallas_call(
        functools.partial(_ret_kernel, seq=seq, hps=hps),
        grid=(batch, ng),
        in_specs=[
            pl.BlockSpec(memory_space=pltpu.SMEM),
            blk(0), blk(ng), blk(2 * ng), blk(3 * ng),
            tab(RET_CHUNK, RET_CHUNK), tab(RET_CHUNK, hd), tab(RET_CHUNK, hd),
        ],
        out_specs=[
            pl.BlockSpec((1, seq, hps * hd), lambda b, h: (b, 0, h)),
            pl.BlockSpec((1, hps, hd, hd), lambda b, h: (b, h, 0, 0)),
        ],
        out_shape=[
            jax.ShapeDtypeStruct((batch, seq, n_heads * hd), BF16),
            jax.ShapeDtypeStruct((batch, n_heads, hd, hd), F32),
        ],
        compiler_params=pltpu.CompilerParams(
            dimension_semantics=("arbitrary", "arbitrary"),
            vmem_limit_bytes=_vmem_limit(est + (16 << 20))),
        name="retention_prompt",
    )(gpow, z3, z3, z3, z3, dec, qd, kd)
    return o.reshape(batch * seq, n_heads * hd), st


def _bf(x):
    return x.astype(BF16).astype(F32)


def _sample_kernel(gam_ref, qkv_ref, zr_ref, k1_ref, k4_ref, k16_ref, v1_ref, v4_ref, v16_ref, st_ref,
                   oa_ref, or_ref, ns_ref, *, n_rh):
    rhd = RET_HEAD_DIM
    dr = n_rh * rhd
    qb = _bf(qkv_ref[0, 0])
    vnb = _bf(qkv_ref[0, 2])
    s_new = jnp.sum(qb * _bf(qkv_ref[0, 1]), axis=-1, keepdims=True)
    m_run = l_run = u_run = None
    for kb, vb in ((k1_ref, v1_ref), (k4_ref, v4_ref), (k16_ref, v16_ref)):
        s = jnp.sum(_bf(kb[0, :, 0]) * qb[None], axis=-1, keepdims=True)
        m = jnp.maximum(jnp.max(s, axis=0), s_new)
        p = jnp.exp(s - m[None])
        pn = jnp.exp(s_new - m)
        l = jnp.sum(p, axis=0) + pn
        u = jnp.sum(_bf(p) * _bf(vb[0, :, 0]), axis=0) + pn * vnb
        if m_run is None:
            m_run, l_run, u_run = m, l, u
        else:
            m_new = jnp.maximum(m_run, m)
            a_old = jnp.exp(m_run - m_new)
            a_new = jnp.exp(m - m_new)
            l_run = a_old * l_run + a_new * l
            u_run = a_old * u_run + a_new * u
            m_run = m_new
    oa_ref[0] = u_run / l_run

    zr = zr_ref[0]
    row0 = lax.broadcasted_iota(jnp.int32, (8, rhd), 0) == 0
    for h in range(n_rh):
        gam = gam_ref[h]
        q = zr[:, h * rhd:(h + 1) * rhd]
        k = zr[:, dr + h * rhd: dr + (h + 1) * rhd]
        v = zr[:, 2 * dr + h * rhd: 2 * dr + (h + 1) * rhd]
        g = zr[:, 3 * dr + h * rhd: 3 * dr + (h + 1) * rhd]
        st = st_ref[0, h]
        q8 = jnp.broadcast_to(q, (8, rhd)).astype(BF16)
        cross = _dot(q8, st.astype(BF16))[:1] * gam
        inner = _bf(jnp.sum(_bf(q) * _bf(k), axis=-1, keepdims=True))
        o = inner * _bf(v) + cross
        k8 = jnp.where(row0, jnp.broadcast_to(k, (8, rhd)), 0.0).astype(BF16)
        v8 = jnp.broadcast_to(v, (8, rhd)).astype(BF16)
        ns_ref[0, h] = gam * st + _dot_tn(k8, v8)
        o = o * lax.rsqrt(jnp.mean(o * o, axis=-1, keepdims=True) + RMS_EPS)
        or_ref[0, :, h * rhd:(h + 1) * rhd] = o * (g * jax.nn.sigmoid(g))


def _sample_mixer(za, zr, cache_k, cache_v, state, layer, *, n_ah, n_rh):
    nl, nb_, n_buf = cache_k.shape[:3]
    ahd, rhd = ATTN_HEAD_DIM, RET_HEAD_DIM
    da = n_ah * ahd
    dr = n_rh * rhd
    assert n_buf == DILATIONS[-1][0]
    span = DIL_BLOCK
    views, specs = [], []
    for cache in (cache_k, cache_v):
        for (_, r) in DILATIONS:
            views.append(cache.reshape(nl * nb_, n_buf // r, r, n_ah, ahd))
            rb = (n_buf // r) // span - 1
            specs.append(pl.BlockSpec((1, span, 1, n_ah, ahd), lambda b, rb=rb: (layer * nb_ + b, rb, 0, 0, 0)))
    ld = jnp.log1p(-jnp.exp2(-5.0 - jnp.arange(n_rh, dtype=F32)))
    gam = jnp.exp(ld)
    st4 = state.reshape(nl * nb_, n_rh, rhd, rhd)
    est = 2 * 6 * span * da * 4 + 4 * n_rh * rhd * rhd * 4 + 8 * span * n_ah * 128 * 4
    oa, orr, ns = pl.pallas_call(
        functools.partial(_sample_kernel, n_rh=n_rh),
        grid=(nb_,),
        in_specs=[
            pl.BlockSpec(memory_space=pltpu.SMEM),
            pl.BlockSpec((1, 3, n_ah, ahd), lambda b: (b, 0, 0, 0)),
            pl.BlockSpec((1, 1, 4 * dr), lambda b: (b, 0, 0)),
            *specs,
            pl.BlockSpec((1, n_rh, rhd, rhd), lambda b: (layer * nb_ + b, 0, 0, 0)),
        ],
        out_specs=[
            pl.BlockSpec((1, n_ah, ahd), lambda b: (b, 0, 0)),
            pl.BlockSpec((1, 1, dr), lambda b: (b, 0, 0)),
            pl.BlockSpec((1, n_rh, rhd, rhd), lambda b: (b, 0, 0, 0)),
        ],
        out_shape=[
            jax.ShapeDtypeStruct((nb_, n_ah, ahd), F32),
            jax.ShapeDtypeStruct((nb_, 1, dr), F32),
            jax.ShapeDtypeStruct((nb_, n_rh, rhd, rhd), F32),
        ],
        compiler_params=pltpu.CompilerParams(
            dimension_semantics=("arbitrary",),
            vmem_limit_bytes=_vmem_limit(est + (16 << 20))),
        name="sample_mixer",
    )(gam, za.reshape(nb_, 3, n_ah, ahd), zr.reshape(nb_, 1, 4 * dr), *views, st4)
    return jnp.concatenate([oa.reshape(nb_, da), orr.reshape(nb_, dr)], axis=-1), ns


def _rot_tables(pos, inv_freq, scales):
    ang = pos.astype(F32)[:, None] * inv_freq[None, :]
    cos, sin = jnp.cos(ang), jnp.sin(ang)
    c_full = jnp.concatenate([cos, cos], axis=-1)
    s_full = jnp.concatenate([-sin, sin], axis=-1)
    cs, ss = [], []
    for sc in scales:
        if sc is None:
            cs.append(jnp.ones_like(c_full))
            ss.append(jnp.zeros_like(s_full))
        else:
            cs.append(c_full * sc)
            ss.append(s_full * sc)
    return jnp.stack(cs), jnp.stack(ss)


def _tables(pos):
    inv_a = ROPE_THETA ** (-jnp.arange(0, ATTN_HEAD_DIM, 2, dtype=F32) / ATTN_HEAD_DIM)
    inv_r = 1.0 / (10000.0 ** jnp.linspace(0.0, 1.0, RET_HEAD_DIM // 2, dtype=F32))
    ta = _rot_tables(pos, inv_a, (ATTN_HEAD_DIM ** -0.5, 1.0, None))
    tr = _rot_tables(pos, inv_r, (1.0, RET_HEAD_DIM ** -0.5, None, None))
    return ta, tr


def kernel(x_prompt, x_sample, cache_attn_k, cache_attn_v, state_ret, g_ffn1, w_ffn1_gate, w_ffn1_up,
           w_ffn1_down, g_mix, w_in, w_out, g_ffn2, w_ffn2_gate, w_ffn2_up, w_ffn2_down, g_final):
    batch, seq, d = x_prompt.shape
    dec_batch, t_new, _ = x_sample.shape
    assert t_new == 1 and dec_batch <= SAMPLE_ROWS
    depth = g_ffn1.shape[0]
    d_attn = d // 2
    d_ret = d // 2
    n_ah = d_attn // ATTN_HEAD_DIM
    n_rh = d_ret // RET_HEAD_DIM
    m = batch * seq
    bm = min(1024, seq)
    ms = SAMPLE_ROWS
    assert seq == DILATIONS[-1][0]

    tabs_a, tabs_r = _tables(jnp.arange(seq))
    tabs_as, tabs_rs = _tables(jnp.full((ms,), PAST_LEN))
    bm_proj = min(512, seq)
    proj_a = dict(col0=0, n=3 * d_attn, hd=ATTN_HEAD_DIM, seg=d_attn, bm=bm_proj)
    proj_r = dict(col0=3 * d_attn, n=4 * d_ret, hd=RET_HEAD_DIM, seg=d_ret, bm=bm_proj)

    xp = x_prompt.reshape(m, d)
    xs = jnp.pad(x_sample.reshape(dec_batch, d), ((0, ms - dec_batch), (0, 0)))
    zas_p, ps, sk, sv, ss = [], [], [], [], []
    for l in range(depth):
        xp, hp, xs, hs = _ffn(xp, xs, g_ffn1[l], w_ffn1_gate, w_ffn1_up, w_ffn1_down, g_mix[l],
                              layer=l, mode="norm", bm=bm)
        za, za_s = _proj(hp, hs, w_in, tabs_a, tabs_as, layer=l, **proj_a)
        zr, zr_s = _proj(hp, hs, w_in, tabs_r, tabs_rs, layer=l, **proj_r)
        oa = _attention(za, batch=batch, seq=seq, n_heads=n_ah)
        orr, st_p = _retention(zr, batch=batch, seq=seq, n_heads=n_rh)
        o_s, st_s = _sample_mixer(za_s[:dec_batch], zr_s[:dec_batch], cache_attn_k, cache_attn_v, state_ret, l,
                                  n_ah=n_ah, n_rh=n_rh)
        o_s = jnp.pad(o_s, ((0, ms - dec_batch), (0, 0))).astype(BF16)
        xp, xs = _out_proj(xp, oa, orr, xs, o_s, w_out, layer=l, bm=bm_proj)
        last = l == depth - 1
        xp, xs = _ffn(xp, xs, g_ffn2[l], w_ffn2_gate, w_ffn2_up, w_ffn2_down, g_final,
                      layer=l, mode="final" if last else "mid", bm=bm)
        zas_p.append(za)
        ps.append(st_p)
        za_s4 = za_s[:dec_batch].reshape(dec_batch, 1, 3, n_ah, ATTN_HEAD_DIM)
        sk.append(za_s4[:, :, 1])
        sv.append(za_s4[:, :, 2])
        ss.append(st_s)

    pk, pv = _kv_outputs(zas_p, n_heads=n_ah)
    kv_shape = (depth, batch, seq, n_ah, ATTN_HEAD_DIM)
    y_prompt = xp.reshape(batch, seq, d)
    y_sample = xs[:dec_batch].reshape(dec_batch, t_new, d)
    return (y_prompt, y_sample, pk.reshape(kv_shape), pv.reshape(kv_shape), jnp.stack(ps),
            jnp.stack(sk), jnp.stack(sv), jnp.stack(ss))
```

```python
import functools

import jax
import jax.numpy as jnp
from jax import lax
from jax.experimental import pallas as pl
from jax.experimental.pallas import tpu as pltpu

F32 = jnp.float32
BF16 = jnp.bfloat16

ATTN_HEAD_DIM = 128
RET_HEAD_DIM = 256
DILATIONS = ((128, 1), (512, 4), (2048, 16))
DIL_BLOCK = 128
RET_CHUNK = 128
ROPE_THETA = 10000.0
RMS_EPS = 1e-6
NEG_INF = -1e30
PAST_LEN = 8192

V7X_VMEM_BYTES = 64 * 1024 * 1024
VMEM_HEADROOM_BYTES = 3 * 1024 * 1024
SAMPLE_ROWS = 16


def _vmem_limit(estimate_bytes):
    return int(min(V7X_VMEM_BYTES - VMEM_HEADROOM_BYTES, max(estimate_bytes, 16 * 1024 * 1024)))


def _rms(x, g):
    return x * lax.rsqrt(jnp.mean(x * x, axis=-1, keepdims=True) + RMS_EPS) * g


def _dot(a, b):
    return jnp.dot(a, b, preferred_element_type=F32)


def _dot_nt(a, b):
    return lax.dot_general(a, b, (((1,), (1,)), ((), ())), preferred_element_type=F32)


def _dot_tn(a, b):
    return lax.dot_general(a, b, (((0,), (0,)), ((), ())), preferred_element_type=F32)


FFN_LOAD_PARTS = 4


def _ffn_kernel(x_hbm, xs_ref, g_ref, wg_ref, wu_ref, wd_ref, gn_ref, *rest, bm, ms, mode, row_chunk, col_chunk):
    if mode == "norm":
        y_hbm, hn_hbm, ys_ref, hns_ref, acc, h_scr, sem = rest
    else:
        y_hbm, ys_ref, acc, h_scr, sem = rest
    i = pl.program_id(0)
    j = pl.program_id(1)
    nj = pl.num_programs(1)
    rows = pl.ds(pl.multiple_of(i * bm, bm), bm)
    d = acc.shape[1]
    part = bm // FFN_LOAD_PARTS
    tile = pl.ds(0, bm)
    extra = pl.ds(bm, ms)

    def for_row_chunks(fn, start, count):
        def body(t, carry):
            fn(pl.ds(pl.multiple_of(start + t * row_chunk, row_chunk), row_chunk))
            return carry
        lax.fori_loop(0, count // row_chunk, body, 0)

    @pl.when(j == 0)
    def _load():
        copies = [
            pltpu.make_async_copy(x_hbm.at[pl.ds(pl.multiple_of(i * bm + p * part, part), part)],
                                  acc.at[pl.ds(p * part, part)], sem.at[p])
            for p in range(FFN_LOAD_PARTS)]
        for cp in copies:
            cp.start()

        @pl.when(i == 0)
        def _():
            xs = xs_ref[...]
            acc[extra, :] = xs
            h_scr[extra, :] = _rms(xs, g_ref[...]).astype(BF16)

        def norm_rows(r):
            h_scr[r, :] = _rms(acc[r, :], g_ref[...]).astype(BF16)
        for p, cp in enumerate(copies):
            cp.wait()
            for_row_chunks(norm_rows, p * part, part)

    def step(n_rows):
        r = pl.ds(0, n_rows)
        h = h_scr[r, :]
        gate = _dot(h, wg_ref[...].astype(BF16))
        up = _dot(h, wu_ref[...].astype(BF16))
        a = ((0.5 * gate) * jax.nn.sigmoid(gate) * up).astype(BF16)
        wd = wd_ref[...].astype(BF16)
        for c0 in range(0, d, col_chunk):
            acc[r, c0:c0 + col_chunk] += _dot(a, wd[:, c0:c0 + col_chunk])

    @pl.when(i == 0)
    def _with_sample_rows():
        step(bm + ms)

    @pl.when(i > 0)
    def _prompt_rows_only():
        step(bm)

    @pl.when(j == nj - 1)
    def _store():
        @pl.when(i == 0)
        def _():
            ys = acc[extra, :]
            if mode == "final":
                ys_ref[...] = _rms(ys, gn_ref[...])
            else:
                ys_ref[...] = ys
            if mode == "norm":
                hns_ref[...] = _rms(ys, gn_ref[...]).astype(BF16)

        if mode == "final":
            def fin_rows(r):
                acc[r, :] = _rms(acc[r, :], gn_ref[...])
            for_row_chunks(fin_rows, 0, bm)
            cp = pltpu.make_async_copy(acc.at[tile], y_hbm.at[rows], sem.at[0])
            cp.start()
            cp.wait()
        else:
            cp = pltpu.make_async_copy(acc.at[tile], y_hbm.at[rows], sem.at[0])
            cp.start()
            if mode == "norm":
                def nrm_rows(r):
                    h_scr[r, :] = _rms(acc[r, :], gn_ref[...]).astype(BF16)
                for_row_chunks(nrm_rows, 0, bm)
                cp2 = pltpu.make_async_copy(h_scr.at[tile], hn_hbm.at[rows], sem.at[1])
                cp2.start()
                cp2.wait()
            cp.wait()


def _ffn(x, xs, g, wg, wu, wd, gn, *, layer, mode, bm, bn=256):
    m, d = x.shape
    ms = xs.shape[0]
    dff = wg.shape[2]
    row_chunk = min(bm // FFN_LOAD_PARTS, 64)
    assert m % bm == 0 and dff % bn == 0 and bm % (FFN_LOAD_PARTS * row_chunk) == 0
    col_chunk = min(d, 512)
    any_spec = pl.BlockSpec(memory_space=pl.ANY)
    small = pl.BlockSpec((ms, d), lambda i, j: (0, 0))
    vec = pl.BlockSpec((1, d), lambda i, j: (0, 0))
    out_shape = [jax.ShapeDtypeStruct((m, d), F32)]
    out_specs = [any_spec]
    if mode == "norm":
        out_shape.append(jax.ShapeDtypeStruct((m, d), BF16))
        out_specs.append(any_spec)
    out_shape.append(jax.ShapeDtypeStruct((ms, d), F32))
    out_specs.append(small)
    if mode == "norm":
        out_shape.append(jax.ShapeDtypeStruct((ms, d), BF16))
        out_specs.append(small)
    est = bm * d * 6 + 3 * 2 * d * bn * 4 + 3 * d * bn * 2 + 4 * bm * bn * 4 + bm * col_chunk * 4
    return pl.pallas_call(
        functools.partial(_ffn_kernel, bm=bm, ms=ms, mode=mode, row_chunk=row_chunk, col_chunk=col_chunk),
        grid=(m // bm, dff // bn),
        in_specs=[
            any_spec,
            small,
            vec,
            pl.BlockSpec((None, d, bn), lambda i, j: (layer, 0, j)),
            pl.BlockSpec((None, d, bn), lambda i, j: (layer, 0, j)),
            pl.BlockSpec((None, bn, d), lambda i, j: (layer, j, 0)),
            vec,
        ],
        out_specs=out_specs,
        out_shape=out_shape,
        scratch_shapes=[
            pltpu.VMEM((bm + ms, d), F32),
            pltpu.VMEM((bm + ms, d), BF16),
            pltpu.SemaphoreType.DMA((FFN_LOAD_PARTS,)),
        ],
        compiler_params=pltpu.CompilerParams(
            dimension_semantics=("arbitrary", "arbitrary"),
            vmem_limit_bytes=_vmem_limit(est + (4 << 20))),
        name=f"ffn_{mode}",
    )(x, xs, g.reshape(1, d), wg, wu, wd, gn.reshape(1, d))


def _next_chunk_weights(w_hbm, w_stage, w_scr, sem, *, layer, col_block0):
    j = pl.program_id(0)
    nj = pl.num_programs(0)
    bn = w_scr.shape[1]

    def w_copy(jj):
        cols = pl.ds(pl.multiple_of((col_block0 + jj) * bn, bn), bn)
        return pltpu.make_async_copy(w_hbm.at[layer, :, cols], w_stage, sem.at[0])

    @pl.when(j == 0)
    def _():
        w_copy(0).start()

    w_copy(j).wait()
    w_scr[...] = w_stage[...].astype(BF16)

    @pl.when(j + 1 < nj)
    def _():
        w_copy(j + 1).start()


def _proj_kernel(h_ref, hs_ref, w_hbm, c_ref, s_ref, cs_ref, ss_ref, o_ref, os_ref, w_stage, w_scr, sem,
                 *, hd, layer, col_block0):
    i = pl.program_id(1)

    def rotate_store(z, cc, ss, out):
        for c0 in range(0, z.shape[1], hd):
            zh = z[:, c0:c0 + hd]
            out[:, c0:c0 + hd] = zh * cc + pltpu.roll(zh, hd // 2, axis=1) * ss

    @pl.when(i == 0)
    def _first():
        _next_chunk_weights(w_hbm, w_stage, w_scr, sem, layer=layer, col_block0=col_block0)
        rotate_store(_dot(hs_ref[...], w_scr[...]), cs_ref[0], ss_ref[0], os_ref)

    rotate_store(_dot(h_ref[...], w_scr[...]), c_ref[0], s_ref[0], o_ref)


def _proj(h, hs, w, tabs, tabs_s, *, layer, col0, n, hd, seg, bm):
    m, d = h.shape
    ms = hs.shape[0]
    ctab, stab = tabs
    p = ctab.shape[1]
    bn = 1024 if seg % 1024 == 0 else 512
    assert m % bm == 0 and n % bn == 0 and col0 % bn == 0 and seg % bn == 0 and p % bm == 0
    npb = p // bm
    cb0 = col0 // bn
    spc = seg // bn
    est = 2 * bm * d * 2 + d * bn * 4 + d * bn * 2 + 4 * bm * bn * 4 + 4 * bm * hd * 4
    tab = pl.BlockSpec((1, bm, hd), lambda j, i: (j // spc, i % npb, 0))
    tab_s = pl.BlockSpec((1, ms, hd), lambda j, i: (j // spc, 0, 0))
    return pl.pallas_call(
        functools.partial(_proj_kernel, hd=hd, layer=layer, col_block0=cb0),
        grid=(n // bn, m // bm),
        in_specs=[
            pl.BlockSpec((bm, d), lambda j, i: (i, 0)),
            pl.BlockSpec((ms, d), lambda j, i: (0, 0)),
            pl.BlockSpec(memory_space=pl.ANY),
            tab, tab, tab_s, tab_s,
        ],
        out_specs=[
            pl.BlockSpec((bm, bn), lambda j, i: (i, j)),
            pl.BlockSpec((ms, bn), lambda j, i: (0, j)),
        ],
        out_shape=[jax.ShapeDtypeStruct((m, n), F32), jax.ShapeDtypeStruct((ms, n), F32)],
        scratch_shapes=[pltpu.VMEM((d, bn), F32), pltpu.VMEM((d, bn), BF16), pltpu.SemaphoreType.DMA((1,))],
        compiler_params=pltpu.CompilerParams(
            dimension_semantics=("arbitrary", "arbitrary"),
            vmem_limit_bytes=_vmem_limit(est + (4 << 20))),
        name=f"proj_rot_hd{hd}",
    )(h, hs, w, ctab, stab, *tabs_s)


def _out_kernel(x_ref, oa_ref, or_ref, xs_ref, os_ref, w_hbm, y_ref, ys_ref, w_stage, w_scr, sem, *, layer):
    i = pl.program_id(1)
    ka = oa_ref.shape[1]

    @pl.when(i == 0)
    def _first():
        _next_chunk_weights(w_hbm, w_stage, w_scr, sem, layer=layer, col_block0=0)
        ys_ref[...] = xs_ref[...] + _dot(os_ref[...], w_scr[...])

    y_ref[...] = x_ref[...] + (_dot(oa_ref[...], w_scr[:ka, :]) + _dot(or_ref[...], w_scr[ka:, :]))


def _out_proj(x, oa, orr, xs, os_, w, *, layer, bm):
    m, d = x.shape
    ms = xs.shape[0]
    ka, kr = oa.shape[1], orr.shape[1]
    bn = 1024 if d % 1024 == 0 else 512
    assert m % bm == 0 and d % bn == 0 and w.shape[1] == ka + kr and os_.shape == (ms, ka + kr)
    est = 2 * bm * (ka + kr) * 2 + (ka + kr) * bn * 4 + (ka + kr) * bn * 2 + 6 * bm * bn * 4
    return pl.pallas_call(
        functools.partial(_out_kernel, layer=layer),
        grid=(d // bn, m // bm),
        in_specs=[
            pl.BlockSpec((bm, bn), lambda j, i: (i, j)),
            pl.BlockSpec((bm, ka), lambda j, i: (i, 0)),
            pl.BlockSpec((bm, kr), lambda j, i: (i, 0)),
            pl.BlockSpec((ms, bn), lambda j, i: (0, j)),
            pl.BlockSpec((ms, ka + kr), lambda j, i: (0, 0)),
            pl.BlockSpec(memory_space=pl.ANY),
        ],
        out_specs=[
            pl.BlockSpec((bm, bn), lambda j, i: (i, j)),
            pl.BlockSpec((ms, bn), lambda j, i: (0, j)),
        ],
        out_shape=[jax.ShapeDtypeStruct((m, d), F32), jax.ShapeDtypeStruct((ms, d), F32)],
        scratch_shapes=[pltpu.VMEM((ka + kr, bn), F32), pltpu.VMEM((ka + kr, bn), BF16),
                        pltpu.SemaphoreType.DMA((1,))],
        compiler_params=pltpu.CompilerParams(
            dimension_semantics=("arbitrary", "arbitrary"),
            vmem_limit_bytes=_vmem_limit(est + (4 << 20))),
        name="out_proj",
    )(x, oa, orr, xs, os_, w)


def _kv_kernel(*refs, n_layers, n_heads, hd):
    k_refs = refs[:n_layers]
    v_refs = refs[n_layers:2 * n_layers]
    ko_ref, vo_ref = refs[2 * n_layers:]
    layer = pl.program_id(0)
    bm = k_refs[0].shape[0]
    for l in range(n_layers):
        @pl.when(layer == l)
        def _(l=l):
            for h in range(n_heads):
                rows = pl.ds(h, bm, stride=n_heads)
                ko_ref[rows, :] = k_refs[l][:, h * hd:(h + 1) * hd]
                vo_ref[rows, :] = v_refs[l][:, h * hd:(h + 1) * hd]


def _kv_outputs(zas, *, n_heads, bm=256):
    n_layers = len(zas)
    m = zas[0].shape[0]
    hd = ATTN_HEAD_DIM
    da = n_heads * hd
    nm = m // bm
    assert m % bm == 0

    def in_spec(l, col):
        return pl.BlockSpec((bm, da), lambda ll, i, l=l: (jnp.where(ll == l, i, jnp.where(ll < l, 0, nm - 1)), col))

    out_spec = pl.BlockSpec((bm * n_heads, hd), lambda ll, i: (ll * nm + i, 0))
    out_sds = jax.ShapeDtypeStruct((n_layers * m * n_heads, hd), F32)
    return pl.pallas_call(
        functools.partial(_kv_kernel, n_layers=n_layers, n_heads=n_heads, hd=hd),
        grid=(n_layers, nm),
        in_specs=[in_spec(l, 1) for l in range(n_layers)] + [in_spec(l, 2) for l in range(n_layers)],
        out_specs=[out_spec, out_spec],
        out_shape=[out_sds, out_sds],
        compiler_params=pltpu.CompilerParams(
            dimension_semantics=("arbitrary", "arbitrary"),
            vmem_limit_bytes=_vmem_limit(2 * 2 * (n_layers + 1) * bm * da * 4 + (8 << 20))),
        name="kv_window_outputs",
    )(*zas, *zas)


def _attn_block(qb, kw, vw, first):
    s = _dot_nt(qb, kw)
    qi = lax.broadcasted_iota(jnp.int32, s.shape, 0)
    kj = lax.broadcasted_iota(jnp.int32, s.shape, 1)
    if first:
        mask = kj <= qi
    else:
        mask = (kj >= qi) & (kj <= qi + DIL_BLOCK)
    s = jnp.where(mask, s, NEG_INF)
    m = jnp.max(s, axis=-1, keepdims=True)
    p = jnp.where(mask, jnp.exp(s - m), 0.0)
    l = jnp.sum(p, axis=-1, keepdims=True)
    u = _dot(p.astype(BF16), vw)
    return u, m, l


def _attn_kernel(q_ref, k_ref, v_ref, o_ref, u_scr, m_scr, l_scr, *, seq):
    blk = DIL_BLOCK
    hd = q_ref.shape[2]

    def load(ref, start, size, stride):
        if stride == 1:
            return ref[0, pl.ds(start, size), :].astype(BF16)
        return ref[0, pl.ds(start, size, stride=stride), :].astype(BF16)

    def block(r, c, n):
        q0 = c + r * n * blk
        qb = load(q_ref, q0, blk, r)
        if n == 0:
            kw = load(k_ref, q0, blk, r)
            vw = load(v_ref, q0, blk, r)
        else:
            k0 = c + r * (n - 1) * blk
            kw = load(k_ref, k0, 2 * blk, r)
            vw = load(v_ref, k0, 2 * blk, r)
        u, m, l = _attn_block(qb, kw, vw, n == 0)
        if r == 1:
            rows = pl.ds(q0, blk)
        else:
            rows = pl.ds(q0, blk, stride=r)
        return rows, u, m, l

    first_r = DILATIONS[0][1]
    last_r = DILATIONS[-1][1]
    for (_, r) in DILATIONS:
        lsub = seq // r
        nb = -(-lsub // blk)
        for c in range(r):
            for n in range(nb):
                rows, u, m, l = block(r, c, n)
                m = jnp.broadcast_to(m, (blk, hd))
                l = jnp.broadcast_to(l, (blk, hd))
                if r == first_r:
                    m_run, l_run, u_run = m, l, u
                else:
                    m_old = m_scr[rows, :]
                    m_run = jnp.maximum(m_old, m)
                    a_old = jnp.exp(m_old - m_run)
                    a_new = jnp.exp(m - m_run)
                    l_run = a_old * l_scr[rows, :] + a_new * l
                    u_run = a_old * u_scr[rows, :] + a_new * u
                if r == last_r:
                    u_scr[rows, :] = u_run / l_run
                else:
                    u_scr[rows, :] = u_run
                    m_scr[rows, :] = m_run
                    l_scr[rows, :] = l_run
    o_ref[0] = u_scr[...].astype(BF16)


def _attention(za, *, batch, seq, n_heads):
    hd = ATTN_HEAD_DIM
    z3 = za.reshape(batch, seq, 3 * n_heads * hd)
    est = 3 * 2 * seq * hd * 4 + 2 * seq * hd * 2 + 3 * seq * hd * 4
    o = pl.pallas_call(
        functools.partial(_attn_kernel, seq=seq),
        grid=(batch, n_heads),
        in_specs=[
            pl.BlockSpec((1, seq, hd), lambda b, h: (b, 0, h)),
            pl.BlockSpec((1, seq, hd), lambda b, h: (b, 0, n_heads + h)),
            pl.BlockSpec((1, seq, hd), lambda b, h: (b, 0, 2 * n_heads + h)),
        ],
        out_specs=pl.BlockSpec((1, seq, hd), lambda b, h: (b, 0, h)),
        out_shape=jax.ShapeDtypeStruct((batch, seq, n_heads * hd), BF16),
        scratch_shapes=[pltpu.VMEM((seq, hd), F32)] * 3,
        compiler_params=pltpu.CompilerParams(
            dimension_semantics=("arbitrary", "arbitrary"),
            vmem_limit_bytes=_vmem_limit(est + (32 << 20))),
        name="dilated_attn_prompt",
    )(z3, z3, z3)
    return o.reshape(batch * seq, n_heads * hd)


RET_HEADS_PER_STEP = 2


def _ret_kernel(gpow_ref, q_ref, k_ref, v_ref, g_ref, dec_ref, qd_ref, kd_ref, o_ref, st_ref, *, seq, hps):
    ck = RET_CHUNK
    hd = RET_HEAD_DIM
    h0 = pl.program_id(1) * hps
    st_ref[...] = jnp.zeros_like(st_ref)

    def chunk(t, carry):
        rows = pl.ds(pl.multiple_of(t * ck, ck), ck)
        for hh in range(hps):
            cols = slice(hh * hd, (hh + 1) * hd)
            q = q_ref[0, rows, cols].astype(BF16)
            k = k_ref[0, rows, cols]
            v = v_ref[0, rows, cols].astype(BF16)
            inner = _dot_nt(q, k.astype(BF16)) * dec_ref[hh]
            intra = _dot(inner.astype(BF16), v)
            st = st_ref[0, hh]
            cross = _dot(q, st.astype(BF16)) * qd_ref[hh]
            o = intra + cross
            kdec = (k * kd_ref[hh]).astype(BF16)
            st_ref[0, hh] = gpow_ref[h0 + hh] * st + _dot_tn(kdec, v)
            o = o * lax.rsqrt(jnp.mean(o * o, axis=-1, keepdims=True) + RMS_EPS)
            g = g_ref[0, rows, cols]
            o_ref[0, rows, cols] = (o * (g * jax.nn.sigmoid(g))).astype(BF16)
        return carry

    lax.fori_loop(0, seq // ck, chunk, 0)


def _ret_tables(n_heads, hd):
    hh = jnp.arange(n_heads, dtype=F32)
    ld = jnp.log1p(-jnp.exp2(-5.0 - hh))
    n = jnp.arange(RET_CHUNK, dtype=F32)
    diff = n[:, None] - n[None, :]
    dec = jnp.where(diff >= 0, jnp.exp(jnp.maximum(diff, 0.0)[None] * ld[:, None, None]), 0.0)
    qd = jnp.exp((n[None, :] + 1.0) * ld[:, None])
    kd = jnp.exp((RET_CHUNK - 1.0 - n)[None, :] * ld[:, None])
    qd = jnp.broadcast_to(qd[:, :, None], (n_heads, RET_CHUNK, hd))
    kd = jnp.broadcast_to(kd[:, :, None], (n_heads, RET_CHUNK, hd))
    gpow = jnp.exp(RET_CHUNK * ld)
    return ld, dec, qd, kd, gpow


def _retention(zr, *, batch, seq, n_heads):
    hd = RET_HEAD_DIM
    hps = RET_HEADS_PER_STEP if n_heads % RET_HEADS_PER_STEP == 0 else 1
    ng = n_heads // hps
    z3 = zr.reshape(batch, seq, 4 * n_heads * hd)
    _, dec, qd, kd, gpow = _ret_tables(n_heads, hd)
    blk = lambda off: pl.BlockSpec((1, seq, hps * hd), lambda b, h: (b, 0, off + h))
    tab = lambda r, c: pl.BlockSpec((hps, r, c), lambda b, h: (h, 0, 0))
    est = 4 * 2 * seq * hps * hd * 4 + 2 * seq * hps * hd * 2 + 2 * hps * hd * hd * 4
    o, st = pl.pallas_call(
        functools.partial(_ret_kernel, seq=seq, hps=hps),
        grid=(batch, ng),
        in_specs=[
            pl.BlockSpec(memory_space=pltpu.SMEM),
            blk(0), blk(ng), blk(2 * ng), blk(3 * ng),
            tab(RET_CHUNK, RET_CHUNK), tab(RET_CHUNK, hd), tab(RET_CHUNK, hd),
        ],
        out_specs=[
            pl.BlockSpec((1, seq, hps * hd), lambda b, h: (b, 0, h)),
            pl.BlockSpec((1, hps, hd, hd), lambda b, h: (b, h, 0, 0)),
        ],
        out_shape=[
            jax.ShapeDtypeStruct((batch, seq, n_heads * hd), BF16),
            jax.ShapeDtypeStruct((batch, n_heads, hd, hd), F32),
        ],
        compiler_params=pltpu.CompilerParams(
            dimension_semantics=("arbitrary", "arbitrary"),
            vmem_limit_bytes=_vmem_limit(est + (16 << 20))),
        name="retention_prompt",
    )(gpow, z3, z3, z3, z3, dec, qd, kd)
    return o.reshape(batch * seq, n_heads * hd), st


def _bf(x):
    return x.astype(BF16).astype(F32)


def _sample_kernel(gam_ref, qkv_ref, zr_ref, k1_ref, k4_ref, k16_ref, v1_ref, v4_ref, v16_ref, st_ref,
                   oa_ref, or_ref, ns_ref, *, n_rh):
    rhd = RET_HEAD_DIM
    dr = n_rh * rhd
    qb = _bf(qkv_ref[0, 0])
    vnb = _bf(qkv_ref[0, 2])
    s_new = jnp.sum(qb * _bf(qkv_ref[0, 1]), axis=-1, keepdims=True)
    m_run = l_run = u_run = None
    for kb, vb in ((k1_ref, v1_ref), (k4_ref, v4_ref), (k16_ref, v16_ref)):
        s = jnp.sum(_bf(kb[0, :, 0]) * qb[None], axis=-1, keepdims=True)
        m = jnp.maximum(jnp.max(s, axis=0), s_new)
        p = jnp.exp(s - m[None])
        pn = jnp.exp(s_new - m)
        l = jnp.sum(p, axis=0) + pn
        u = jnp.sum(_bf(p) * _bf(vb[0, :, 0]), axis=0) + pn * vnb
        if m_run is None:
            m_run, l_run, u_run = m, l, u
        else:
            m_new = jnp.maximum(m_run, m)
            a_old = jnp.exp(m_run - m_new)
            a_new = jnp.exp(m - m_new)
            l_run = a_old * l_run + a_new * l
            u_run = a_old * u_run + a_new * u
            m_run = m_new
    oa_ref[0] = u_run / l_run

    zr = zr_ref[0]
    row0 = lax.broadcasted_iota(jnp.int32, (8, rhd), 0) == 0
    for h in range(n_rh):
        gam = gam_ref[h]
        q = zr[:, h * rhd:(h + 1) * rhd]
        k = zr[:, dr + h * rhd: dr + (h + 1) * rhd]
        v = zr[:, 2 * dr + h * rhd: 2 * dr + (h + 1) * rhd]
        g = zr[:, 3 * dr + h * rhd: 3 * dr + (h + 1) * rhd]
        st = st_ref[0, h]
        q8 = jnp.broadcast_to(q, (8, rhd)).astype(BF16)
        cross = _dot(q8, st.astype(BF16))[:1] * gam
        inner = _bf(jnp.sum(_bf(q) * _bf(k), axis=-1, keepdims=True))
        o = inner * _bf(v) + cross
        k8 = jnp.where(row0, jnp.broadcast_to(k, (8, rhd)), 0.0).astype(BF16)
        v8 = jnp.broadcast_to(v, (8, rhd)).astype(BF16)
        ns_ref[0, h] = gam * st + _dot_tn(k8, v8)
        o = o * lax.rsqrt(jnp.mean(o * o, axis=-1, keepdims=True) + RMS_EPS)
        or_ref[0, :, h * rhd:(h + 1) * rhd] = o * (g * jax.nn.sigmoid(g))


def _sample_mixer(za, zr, cache_k, cache_v, state, layer, *, n_ah, n_rh):
    nl, nb_, n_buf = cache_k.shape[:3]
    ahd, rhd = ATTN_HEAD_DIM, RET_HEAD_DIM
    da = n_ah * ahd
    dr = n_rh * rhd
    assert n_buf == DILATIONS[-1][0]
    span = DIL_BLOCK
    views, specs = [], []
    for cache in (cache_k, cache_v):
        for (_, r) in DILATIONS:
            views.append(cache.reshape(nl * nb_, n_buf // r, r, n_ah, ahd))
            rb = (n_buf // r) // span - 1
            specs.append(pl.BlockSpec((1, span, 1, n_ah, ahd), lambda b, rb=rb: (layer * nb_ + b, rb, 0, 0, 0)))
    ld = jnp.log1p(-jnp.exp2(-5.0 - jnp.arange(n_rh, dtype=F32)))
    gam = jnp.exp(ld)
    st4 = state.reshape(nl * nb_, n_rh, rhd, rhd)
    est = 2 * 6 * span * da * 4 + 4 * n_rh * rhd * rhd * 4 + 8 * span * n_ah * 128 * 4
    oa, orr, ns = pl.pallas_call(
        functools.partial(_sample_kernel, n_rh=n_rh),
        grid=(nb_,),
        in_specs=[
            pl.BlockSpec(memory_space=pltpu.SMEM),
            pl.BlockSpec((1, 3, n_ah, ahd), lambda b: (b, 0, 0, 0)),
            pl.BlockSpec((1, 1, 4 * dr), lambda b: (b, 0, 0)),
            *specs,
            pl.BlockSpec((1, n_rh, rhd, rhd), lambda b: (layer * nb_ + b, 0, 0, 0)),
        ],
        out_specs=[
            pl.BlockSpec((1, n_ah, ahd), lambda b: (b, 0, 0)),
            pl.BlockSpec((1, 1, dr), lambda b: (b, 0, 0)),
            pl.BlockSpec((1, n_rh, rhd, rhd), lambda b: (b, 0, 0, 0)),
        ],
        out_shape=[
            jax.ShapeDtypeStruct((nb_, n_ah, ahd), F32),
            jax.ShapeDtypeStruct((nb_, 1, dr), F32),
            jax.ShapeDtypeStruct((nb_, n_rh, rhd, rhd), F32),
        ],
        compiler_params=pltpu.CompilerParams(
            dimension_semantics=("arbitrary",),
            vmem_limit_bytes=_vmem_limit(est + (16 << 20))),
        name="sample_mixer",
    )(gam, za.reshape(nb_, 3, n_ah, ahd), zr.reshape(nb_, 1, 4 * dr), *views, st4)
    return jnp.concatenate([oa.reshape(nb_, da), orr.reshape(nb_, dr)], axis=-1), ns


def _rot_tables(pos, inv_freq, scales):
    ang = pos.astype(F32)[:, None] * inv_freq[None, :]
    cos, sin = jnp.cos(ang), jnp.sin(ang)
    c_full = jnp.concatenate([cos, cos], axis=-1)
    s_full = jnp.concatenate([-sin, sin], axis=-1)
    cs, ss = [], []
    for sc in scales:
        if sc is None:
            cs.append(jnp.ones_like(c_full))
            ss.append(jnp.zeros_like(s_full))
        else:
            cs.append(c_full * sc)
            ss.append(s_full * sc)
    return jnp.stack(cs), jnp.stack(ss)


def _tables(pos):
    inv_a = ROPE_THETA ** (-jnp.arange(0, ATTN_HEAD_DIM, 2, dtype=F32) / ATTN_HEAD_DIM)
    inv_r = 1.0 / (10000.0 ** jnp.linspace(0.0, 1.0, RET_HEAD_DIM // 2, dtype=F32))
    ta = _rot_tables(pos, inv_a, (ATTN_HEAD_DIM ** -0.5, 1.0, None))
    tr = _rot_tables(pos, inv_r, (1.0, RET_HEAD_DIM ** -0.5, None, None))
    return ta, tr


def kernel(x_prompt, x_sample, cache_attn_k, cache_attn_v, state_ret, g_ffn1, w_ffn1_gate, w_ffn1_up,
           w_ffn1_down, g_mix, w_in, w_out, g_ffn2, w_ffn2_gate, w_ffn2_up, w_ffn2_down, g_final):
    batch, seq, d = x_prompt.shape
    dec_batch, t_new, _ = x_sample.shape
    assert t_new == 1 and dec_batch <= SAMPLE_ROWS
    depth = g_ffn1.shape[0]
    d_attn = d // 2
    d_ret = d // 2
    n_ah = d_attn // ATTN_HEAD_DIM
    n_rh = d_ret // RET_HEAD_DIM
    m = batch * seq
    bm = min(1024, seq)
    ms = SAMPLE_ROWS
    assert seq == DILATIONS[-1][0]

    tabs_a, tabs_r = _tables(jnp.arange(seq))
    tabs_as, tabs_rs = _tables(jnp.full((ms,), PAST_LEN))
    bm_proj = min(512, seq)
    proj_a = dict(col0=0, n=3 * d_attn, hd=ATTN_HEAD_DIM, seg=d_attn, bm=bm_proj)
    proj_r = dict(col0=3 * d_attn, n=4 * d_ret, hd=RET_HEAD_DIM, seg=d_ret, bm=bm_proj)

    xp = x_prompt.reshape(m, d)
    xs = jnp.pad(x_sample.reshape(dec_batch, d), ((0, ms - dec_batch), (0, 0)))
    zas_p, ps, sk, sv, ss = [], [], [], [], []
    for l in range(depth):
        xp, hp, xs, hs = _ffn(xp, xs, g_ffn1[l], w_ffn1_gate, w_ffn1_up, w_ffn1_down, g_mix[l],
                              layer=l, mode="norm", bm=bm)
        za, za_s = _proj(hp, hs, w_in, tabs_a, tabs_as, layer=l, **proj_a)
        zr, zr_s = _proj(hp, hs, w_in, tabs_r, tabs_rs, layer=l, **proj_r)
        oa = _attention(za, batch=batch, seq=seq, n_heads=n_ah)
        orr, st_p = _retention(zr, batch=batch, seq=seq, n_heads=n_rh)
        o_s, st_s = _sample_mixer(za_s[:dec_batch], zr_s[:dec_batch], cache_attn_k, cache_attn_v, state_ret, l,
                                  n_ah=n_ah, n_rh=n_rh)
        o_s = jnp.pad(o_s, ((0, ms - dec_batch), (0, 0))).astype(BF16)
        xp, xs = _out_proj(xp, oa, orr, xs, o_s, w_out, layer=l, bm=bm_proj)
        last = l == depth - 1
        xp, xs = _ffn(xp, xs, g_ffn2[l], w_ffn2_gate, w_ffn2_up, w_ffn2_down, g_final,
                      layer=l, mode="final" if last else "mid", bm=bm)
        zas_p.append(za)
        ps.append(st_p)
        za_s4 = za_s[:dec_batch].reshape(dec_batch, 1, 3, n_ah, ATTN_HEAD_DIM)
        sk.append(za_s4[:, :, 1])
        sv.append(za_s4[:, :, 2])
        ss.append(st_s)

    pk, pv = _kv_outputs(zas_p, n_heads=n_ah)
    kv_shape = (depth, batch, seq, n_ah, ATTN_HEAD_DIM)
    y_prompt = xp.reshape(batch, seq, d)
    y_sample = xs[:dec_batch].reshape(dec_batch, t_new, d)
    return (y_prompt, y_sample, pk.reshape(kv_shape), pv.reshape(kv_shape), jnp.stack(ps),
            jnp.stack(sk), jnp.stack(sv), jnp.stack(ss))
```

```python
import functools

import jax
import jax.numpy as jnp
from jax import lax
from jax.experimental import pallas as pl
from jax.experimental.pallas import tpu as pltpu

F32 = jnp.float32
BF16 = jnp.bfloat16

ATTN_HEAD_DIM = 128
RET_HEAD_DIM = 256
DILATIONS = ((128, 1), (512, 4), (2048, 16))
DIL_BLOCK = 128
RET_CHUNK = 128
ROPE_THETA = 10000.0
RMS_EPS = 1e-6
NEG_INF = -1e30
PAST_LEN = 8192

V7X_VMEM_BYTES = 64 * 1024 * 1024
VMEM_HEADROOM_BYTES = 3 * 1024 * 1024
SAMPLE_ROWS = 16


def _vmem_limit(estimate_bytes):
    return int(min(V7X_VMEM_BYTES - VMEM_HEADROOM_BYTES, max(estimate_bytes, 16 * 1024 * 1024)))


def _rms(x, g):
    return x * lax.rsqrt(jnp.mean(x * x, axis=-1, keepdims=True) + RMS_EPS) * g


def _dot(a, b):
    return jnp.dot(a, b, preferred_element_type=F32)


def _dot_nt(a, b):
    return lax.dot_general(a, b, (((1,), (1,)), ((), ())), preferred_element_type=F32)


def _dot_tn(a, b):
    return lax.dot_general(a, b, (((0,), (0,)), ((), ())), preferred_element_type=F32)


FFN_LOAD_PARTS = 4


def _ffn_kernel(x_hbm, xs_ref, g_ref, wg_ref, wu_ref, wd_ref, gn_ref, *rest, bm, ms, mode, row_chunk, col_chunk):
    if mode == "norm":
        y_hbm, hn_hbm, ys_ref, hns_ref, acc, h_scr, sem = rest
    else:
        y_hbm, ys_ref, acc, h_scr, sem = rest
    i = pl.program_id(0)
    j = pl.program_id(1)
    nj = pl.num_programs(1)
    rows = pl.ds(pl.multiple_of(i * bm, bm), bm)
    d = acc.shape[1]
    part = bm // FFN_LOAD_PARTS
    tile = pl.ds(0, bm)
    extra = pl.ds(bm, ms)

    def for_row_chunks(fn, start, count):
        def body(t, carry):
            fn(pl.ds(pl.multiple_of(start + t * row_chunk, row_chunk), row_chunk))
            return carry
        lax.fori_loop(0, count // row_chunk, body, 0)

    @pl.when(j == 0)
    def _load():
        copies = [
            pltpu.make_async_copy(x_hbm.at[pl.ds(pl.multiple_of(i * bm + p * part, part), part)],
                                  acc.at[pl.ds(p * part, part)], sem.at[p])
            for p in range(FFN_LOAD_PARTS)]
        for cp in copies:
            cp.start()

        @pl.when(i == 0)
        def _():
            xs = xs_ref[...]
            acc[extra, :] = xs
            h_scr[extra, :] = _rms(xs, g_ref[...]).astype(BF16)

        def norm_rows(r):
            h_scr[r, :] = _rms(acc[r, :], g_ref[...]).astype(BF16)
        for p, cp in enumerate(copies):
            cp.wait()
            for_row_chunks(norm_rows, p * part, part)

    def step(n_rows):
        r = pl.ds(0, n_rows)
        h = h_scr[r, :]
        gate = _dot(h, wg_ref[...].astype(BF16))
        up = _dot(h, wu_ref[...].astype(BF16))
        a = ((0.5 * gate) * jax.nn.sigmoid(gate) * up).astype(BF16)
        wd = wd_ref[...].astype(BF16)
        for c0 in range(0, d, col_chunk):
            acc[r, c0:c0 + col_chunk] += _dot(a, wd[:, c0:c0 + col_chunk])

    @pl.when(i == 0)
    def _with_sample_rows():
        step(bm + ms)

    @pl.when(i > 0)
    def _prompt_rows_only():
        step(bm)

    @pl.when(j == nj - 1)
    def _store():
        @pl.when(i == 0)
        def _():
            ys = acc[extra, :]
            if mode == "final":
                ys_ref[...] = _rms(ys, gn_ref[...])
            else:
                ys_ref[...] = ys
            if mode == "norm":
                hns_ref[...] = _rms(ys, gn_ref[...]).astype(BF16)

        if mode == "final":
            def fin_rows(r):
                acc[r, :] = _rms(acc[r, :], gn_ref[...])
            for_row_chunks(fin_rows, 0, bm)
            cp = pltpu.make_async_copy(acc.at[tile], y_hbm.at[rows], sem.at[0])
            cp.start()
            cp.wait()
        else:
            cp = pltpu.make_async_copy(acc.at[tile], y_hbm.at[rows], sem.at[0])
            cp.start()
            if mode == "norm":
                def nrm_rows(r):
                    h_scr[r, :] = _rms(acc[r, :], gn_ref[...]).astype(BF16)
                for_row_chunks(nrm_rows, 0, bm)
                cp2 = pltpu.make_async_copy(h_scr.at[tile], hn_hbm.at[rows], sem.at[1])
                cp2.start()
                cp2.wait()
            cp.wait()


def _ffn(x, xs, g, wg, wu, wd, gn, *, layer, mode, bm, bn=256):
    m, d = x.shape
    ms = xs.shape[0]
    dff = wg.shape[2]
    row_chunk = min(bm // FFN_LOAD_PARTS, 64)
    assert m % bm == 0 and dff % bn == 0 and bm % (FFN_LOAD_PARTS * row_chunk) == 0
    col_chunk = min(d, 512)
    any_spec = pl.BlockSpec(memory_space=pl.ANY)
    small = pl.BlockSpec((ms, d), lambda i, j: (0, 0))
    vec = pl.BlockSpec((1, d), lambda i, j: (0, 0))
    out_shape = [jax.ShapeDtypeStruct((m, d), F32)]
    out_specs = [any_spec]
    if mode == "norm":
        out_shape.append(jax.ShapeDtypeStruct((m, d), BF16))
        out_specs.append(any_spec)
    out_shape.append(jax.ShapeDtypeStruct((ms, d), F32))
    out_specs.append(small)
    if mode == "norm":
        out_shape.append(jax.ShapeDtypeStruct((ms, d), BF16))
        out_specs.append(small)
    est = bm * d * 6 + 3 * 2 * d * bn * 4 + 3 * d * bn * 2 + 4 * bm * bn * 4 + bm * col_chunk * 4
    return pl.pallas_call(
        functools.partial(_ffn_kernel, bm=bm, ms=ms, mode=mode, row_chunk=row_chunk, col_chunk=col_chunk),
        grid=(m // bm, dff // bn),
        in_specs=[
            any_spec,
            small,
            vec,
            pl.BlockSpec((None, d, bn), lambda i, j: (layer, 0, j)),
            pl.BlockSpec((None, d, bn), lambda i, j: (layer, 0, j)),
            pl.BlockSpec((None, bn, d), lambda i, j: (layer, j, 0)),
            vec,
        ],
        out_specs=out_specs,
        out_shape=out_shape,
        scratch_shapes=[
            pltpu.VMEM((bm + ms, d), F32),
            pltpu.VMEM((bm + ms, d), BF16),
            pltpu.SemaphoreType.DMA((FFN_LOAD_PARTS,)),
        ],
        compiler_params=pltpu.CompilerParams(
            dimension_semantics=("arbitrary", "arbitrary"),
            vmem_limit_bytes=_vmem_limit(est + (4 << 20))),
        name=f"ffn_{mode}",
    )(x, xs, g.reshape(1, d), wg, wu, wd, gn.reshape(1, d))


def _next_chunk_weights(w_hbm, w_stage, w_scr, sem, *, layer, col_block0):
    j = pl.program_id(0)
    nj = pl.num_programs(0)
    bn = w_scr.shape[1]

    def w_copy(jj):
        cols = pl.ds(pl.multiple_of((col_block0 + jj) * bn, bn), bn)
        return pltpu.make_async_copy(w_hbm.at[layer, :, cols], w_stage, sem.at[0])

    @pl.when(j == 0)
    def _():
        w_copy(0).start()

    w_copy(j).wait()
    w_scr[...] = w_stage[...].astype(BF16)

    @pl.when(j + 1 < nj)
    def _():
        w_copy(j + 1).start()


def _proj_kernel(h_ref, hs_ref, w_hbm, c_ref, s_ref, cs_ref, ss_ref, o_ref, os_ref, w_stage, w_scr, sem,
                 *, hd, layer, col_block0):
    i = pl.program_id(1)

    def rotate_store(z, cc, ss, out):
        for c0 in range(0, z.shape[1], hd):
            zh = z[:, c0:c0 + hd]
            out[:, c0:c0 + hd] = zh * cc + pltpu.roll(zh, hd // 2, axis=1) * ss

    @pl.when(i == 0)
    def _first():
        _next_chunk_weights(w_hbm, w_stage, w_scr, sem, layer=layer, col_block0=col_block0)
        rotate_store(_dot(hs_ref[...], w_scr[...]), cs_ref[0], ss_ref[0], os_ref)

    rotate_store(_dot(h_ref[...], w_scr[...]), c_ref[0], s_ref[0], o_ref)


def _proj(h, hs, w, tabs, tabs_s, *, layer, col0, n, hd, seg, bm):
    m, d = h.shape
    ms = hs.shape[0]
    ctab, stab = tabs
    p = ctab.shape[1]
    bn = 1024 if seg % 1024 == 0 else 512
    assert m % bm == 0 and n % bn == 0 and col0 % bn == 0 and seg % bn == 0 and p % bm == 0
    npb = p // bm
    cb0 = col0 // bn
    spc = seg // bn
    est = 2 * bm * d * 2 + d * bn * 4 + d * bn * 2 + 4 * bm * bn * 4 + 4 * bm * hd * 4
    tab = pl.BlockSpec((1, bm, hd), lambda j, i: (j // spc, i % npb, 0))
    tab_s = pl.BlockSpec((1, ms, hd), lambda j, i: (j // spc, 0, 0))
    return pl.pallas_call(
        functools.partial(_proj_kernel, hd=hd, layer=layer, col_block0=cb0),
        grid=(n // bn, m // bm),
        in_specs=[
            pl.BlockSpec((bm, d), lambda j, i: (i, 0)),
            pl.BlockSpec((ms, d), lambda j, i: (0, 0)),
            pl.BlockSpec(memory_space=pl.ANY),
            tab, tab, tab_s, tab_s,
        ],
        out_specs=[
            pl.BlockSpec((bm, bn), lambda j, i: (i, j)),
            pl.BlockSpec((ms, bn), lambda j, i: (0, j)),
        ],
        out_shape=[jax.ShapeDtypeStruct((m, n), F32), jax.ShapeDtypeStruct((ms, n), F32)],
        scratch_shapes=[pltpu.VMEM((d, bn), F32), pltpu.VMEM((d, bn), BF16), pltpu.SemaphoreType.DMA((1,))],
        compiler_params=pltpu.CompilerParams(
            dimension_semantics=("arbitrary", "arbitrary"),
            vmem_limit_bytes=_vmem_limit(est + (4 << 20))),
        name=f"proj_rot_hd{hd}",
    )(h, hs, w, ctab, stab, *tabs_s)


def _out_kernel(x_ref, oa_ref, or_ref, xs_ref, os_ref, w_hbm, y_ref, ys_ref, w_stage, w_scr, sem, *, layer):
    i = pl.program_id(1)
    ka = oa_ref.shape[1]

    @pl.when(i == 0)
    def _first():
        _next_chunk_weights(w_hbm, w_stage, w_scr, sem, layer=layer, col_block0=0)
        ys_ref[...] = xs_ref[...] + _dot(os_ref[...], w_scr[...])

    y_ref[...] = x_ref[...] + (_dot(oa_ref[...], w_scr[:ka, :]) + _dot(or_ref[...], w_scr[ka:, :]))


def _out_proj(x, oa, orr, xs, os_, w, *, layer, bm):
    m, d = x.shape
    ms = xs.shape[0]
    ka, kr = oa.shape[1], orr.shape[1]
    bn = 1024 if d % 1024 == 0 else 512
    assert m % bm == 0 and d % bn == 0 and w.shape[1] == ka + kr and os_.shape == (ms, ka + kr)
    est = 2 * bm * (ka + kr) * 2 + (ka + kr) * bn * 4 + (ka + kr) * bn * 2 + 6 * bm * bn * 4
    return pl.pallas_call(
        functools.partial(_out_kernel, layer=layer),
        grid=(d // bn, m // bm),
        in_specs=[
            pl.BlockSpec((bm, bn), lambda j, i: (i, j)),
            pl.BlockSpec((bm, ka), lambda j, i: (i, 0)),
            pl.BlockSpec((bm, kr), lambda j, i: (i, 0)),
            pl.BlockSpec((ms, bn), lambda j, i: (0, j)),
            pl.BlockSpec((ms, ka + kr), lambda j, i: (0, 0)),
            pl.BlockSpec(memory_space=pl.ANY),
        ],
        out_specs=[
            pl.BlockSpec((bm, bn), lambda j, i: (i, j)),
            pl.BlockSpec((ms, bn), lambda j, i: (0, j)),
        ],
        out_shape=[jax.ShapeDtypeStruct((m, d), F32), jax.ShapeDtypeStruct((ms, d), F32)],
        scratch_shapes=[pltpu.VMEM((ka + kr, bn), F32), pltpu.VMEM((ka + kr, bn), BF16),
                        pltpu.SemaphoreType.DMA((1,))],
        compiler_params=pltpu.CompilerParams(
            dimension_semantics=("arbitrary", "arbitrary"),
            vmem_limit_bytes=_vmem_limit(est + (4 << 20))),
        name="out_proj",
    )(x, oa, orr, xs, os_, w)


def _kv_kernel(*refs, n_layers, n_heads, hd):
    k_refs = refs[:n_layers]
    v_refs = refs[n_layers:2 * n_layers]
    ko_ref, vo_ref = refs[2 * n_layers:]
    layer = pl.program_id(0)
    bm = k_refs[0].shape[0]
    for l in range(n_layers):
        @pl.when(layer == l)
        def _(l=l):
            for h in range(n_heads):
                rows = pl.ds(h, bm, stride=n_heads)
                ko_ref[rows, :] = k_refs[l][:, h * hd:(h + 1) * hd]
                vo_ref[rows, :] = v_refs[l][:, h * hd:(h + 1) * hd]


def _kv_outputs(zas, *, n_heads, bm=256):
    n_layers = len(zas)
    m = zas[0].shape[0]
    hd = ATTN_HEAD_DIM
    da = n_heads * hd
    nm = m // bm
    assert m % bm == 0

    def in_spec(l, col):
        return pl.BlockSpec((bm, da), lambda ll, i, l=l: (jnp.where(ll == l, i, jnp.where(ll < l, 0, nm - 1)), col))

    out_spec = pl.BlockSpec((bm * n_heads, hd), lambda ll, i: (ll * nm + i, 0))
    out_sds = jax.ShapeDtypeStruct((n_layers * m * n_heads, hd), F32)
    return pl.pallas_call(
        functools.partial(_kv_kernel, n_layers=n_layers, n_heads=n_heads, hd=hd),
        grid=(n_layers, nm),
        in_specs=[in_spec(l, 1) for l in range(n_layers)] + [in_spec(l, 2) for l in range(n_layers)],
        out_specs=[out_spec, out_spec],
        out_shape=[out_sds, out_sds],
        compiler_params=pltpu.CompilerParams(
            dimension_semantics=("arbitrary", "arbitrary"),
            vmem_limit_bytes=_vmem_limit(2 * 2 * (n_layers + 1) * bm * da * 4 + (8 << 20))),
        name="kv_window_outputs",
    )(*zas, *zas)


def _attn_block(qb, kw, vw, first):
    s = _dot_nt(qb, kw)
    qi = lax.broadcasted_iota(jnp.int32, s.shape, 0)
    kj = lax.broadcasted_iota(jnp.int32, s.shape, 1)
    if first:
        mask = kj <= qi
    else:
        mask = (kj >= qi) & (kj <= qi + DIL_BLOCK)
    s = jnp.where(mask, s, NEG_INF)
    m = jnp.max(s, axis=-1, keepdims=True)
    p = jnp.where(mask, jnp.exp(s - m), 0.0)
    l = jnp.sum(p, axis=-1, keepdims=True)
    u = _dot(p.astype(BF16), vw)
    return u, m, l


def _attn_kernel(q_ref, k_ref, v_ref, o_ref, qa, ka, va, ua, ma, la, *, seq):
    blk = DIL_BLOCK
    hd = q_ref.shape[2]
    (_, r_near), (_, r_mid), (_, r_far) = DILATIONS
    assert r_near == 1 and r_far % r_mid == 0
    sub = r_far // r_mid
    seg = seq // r_mid

    def bf(x):
        return x.astype(BF16)

    def stats(qb, kw, vw, first):
        u, m, l = _attn_block(bf(qb), bf(kw), bf(vw), first)
        return u, jnp.broadcast_to(m, (blk, hd)), jnp.broadcast_to(l, (blk, hd))

    def merge(u_old, m_old, l_old, u, m, l):
        m_run = jnp.maximum(m_old, m)
        a_old = jnp.exp(m_old - m_run)
        a_new = jnp.exp(m - m_run)
        return a_old * u_old + a_new * u, m_run, a_old * l_old + a_new * l

    for c in range(r_mid):
        src = pl.ds(c, seg, stride=r_mid)
        dst = pl.ds(c * seg, seg)
        qa[dst, :] = q_ref[0, src, :]
        ka[dst, :] = k_ref[0, src, :]
        va[dst, :] = v_ref[0, src, :]

    n_far = (seq // r_far) // blk
    for c in range(r_mid):
        for e in range(sub):
            for n in range(n_far):
                q_rows = pl.ds(c * seg + e + sub * n * blk, blk, stride=sub)
                if n == 0:
                    k_rows = q_rows
                else:
                    k_rows = pl.ds(c * seg + e + sub * (n - 1) * blk, 2 * blk, stride=sub)
                u, m, l = stats(qa[q_rows, :], ka[k_rows, :], va[k_rows, :], n == 0)
                ua[q_rows, :] = u
                ma[q_rows, :] = m
                la[q_rows, :] = l

    n_mid = seg // blk
    for c in range(r_mid):
        for n in range(n_mid):
            q_rows = pl.ds(c * seg + n * blk, blk)
            k_rows = q_rows if n == 0 else pl.ds(c * seg + (n - 1) * blk, 2 * blk)
            u, m, l = stats(qa[q_rows, :], ka[k_rows, :], va[k_rows, :], n == 0)
            u, m, l = merge(ua[q_rows, :], ma[q_rows, :], la[q_rows, :], u, m, l)
            ua[q_rows, :] = u
            ma[q_rows, :] = m
            la[q_rows, :] = l

    ut, mt, lt = qa, ka, va
    for c in range(r_mid):
        src = pl.ds(c * seg, seg)
        dst = pl.ds(c, seg, stride=r_mid)
        ut[dst, :] = ua[src, :]
        mt[dst, :] = ma[src, :]
        lt[dst, :] = la[src, :]

    for n in range(seq // blk):
        q_rows = pl.ds(n * blk, blk)
        k_rows = q_rows if n == 0 else pl.ds((n - 1) * blk, 2 * blk)
        u, m, l = stats(q_ref[0, q_rows, :], k_ref[0, k_rows, :], v_ref[0, k_rows, :], n == 0)
        u, m, l = merge(ut[q_rows, :], mt[q_rows, :], lt[q_rows, :], u, m, l)
        o_ref[0, q_rows, :] = (u / l).astype(BF16)


def _attention(za, *, batch, seq, n_heads):
    hd = ATTN_HEAD_DIM
    z3 = za.reshape(batch, seq, 3 * n_heads * hd)
    est = 3 * 2 * seq * hd * 4 + 2 * seq * hd * 2 + 6 * seq * hd * 4
    o = pl.pallas_call(
        functools.partial(_attn_kernel, seq=seq),
        grid=(batch, n_heads),
        in_specs=[
            pl.BlockSpec((1, seq, hd), lambda b, h: (b, 0, h)),
            pl.BlockSpec((1, seq, hd), lambda b, h: (b, 0, n_heads + h)),
            pl.BlockSpec((1, seq, hd), lambda b, h: (b, 0, 2 * n_heads + h)),
        ],
        out_specs=pl.BlockSpec((1, seq, hd), lambda b, h: (b, 0, h)),
        out_shape=jax.ShapeDtypeStruct((batch, seq, n_heads * hd), BF16),
        scratch_shapes=[pltpu.VMEM((seq, hd), F32)] * 6,
        compiler_params=pltpu.CompilerParams(
            dimension_semantics=("arbitrary", "arbitrary"),
            vmem_limit_bytes=_vmem_limit(est + (32 << 20))),
        name="dilated_attn_prompt",
    )(z3, z3, z3)
    return o.reshape(batch * seq, n_heads * hd)


RET_HEADS_PER_STEP = 2


def _ret_kernel(gpow_ref, q_ref, k_ref, v_ref, g_ref, dec_ref, qd_ref, kd_ref, o_ref, st_ref, *, seq, hps):
    ck = RET_CHUNK
    hd = RET_HEAD_DIM
    h0 = pl.program_id(1) * hps
    st_ref[...] = jnp.zeros_like(st_ref)

    def chunk(t, carry):
        rows = pl.ds(pl.multiple_of(t * ck, ck), ck)
        for hh in range(hps):
            cols = slice(hh * hd, (hh + 1) * hd)
            q = q_ref[0, rows, cols].astype(BF16)
            k = k_ref[0, rows, cols]
            v = v_ref[0, rows, cols].astype(BF16)
            inner = _dot_nt(q, k.astype(BF16)) * dec_ref[hh]
            intra = _dot(inner.astype(BF16), v)
            st = st_ref[0, hh]
            cross = _dot(q, st.astype(BF16)) * qd_ref[hh]
            o = intra + cross
            kdec = (k * kd_ref[hh]).astype(BF16)
            st_ref[0, hh] = gpow_ref[h0 + hh] * st + _dot_tn(kdec, v)
            o = o * lax.rsqrt(jnp.mean(o * o, axis=-1, keepdims=True) + RMS_EPS)
            g = g_ref[0, rows, cols]
            o_ref[0, rows, cols] = (o * (g * jax.nn.sigmoid(g))).astype(BF16)
        return carry

    lax.fori_loop(0, seq // ck, chunk, 0, unroll=2)


def _ret_tables(n_heads, hd):
    hh = jnp.arange(n_heads, dtype=F32)
    ld = jnp.log1p(-jnp.exp2(-5.0 - hh))
    n = jnp.arange(RET_CHUNK, dtype=F32)
    diff = n[:, None] - n[None, :]
    dec = jnp.where(diff >= 0, jnp.exp(jnp.maximum(diff, 0.0)[None] * ld[:, None, None]), 0.0)
    qd = jnp.exp((n[None, :] + 1.0) * ld[:, None])
    kd = jnp.exp((RET_CHUNK - 1.0 - n)[None, :] * ld[:, None])
    qd = jnp.broadcast_to(qd[:, :, None], (n_heads, RET_CHUNK, hd))
    kd = jnp.broadcast_to(kd[:, :, None], (n_heads, RET_CHUNK, hd))
    gpow = jnp.exp(RET_CHUNK * ld)
    return ld, dec, qd, kd, gpow


def _retention(zr, *, batch, seq, n_heads):
    hd = RET_HEAD_DIM
    hps = RET_HEADS_PER_STEP if n_heads % RET_HEADS_PER_STEP == 0 else 1
    ng = n_heads // hps
    z3 = zr.reshape(batch, seq, 4 * n_heads * hd)
    _, dec, qd, kd, gpow = _ret_tables(n_heads, hd)
    blk = lambda off: pl.BlockSpec((1, seq, hps * hd), lambda b, h: (b, 0, off + h))
    tab = lambda r, c: pl.BlockSpec((hps, r, c), lambda b, h: (h, 0, 0))
    est = 4 * 2 * seq * hps * hd * 4 + 2 * seq * hps * hd * 2 + 2 * hps * hd * hd * 4
    o, st = pl.pallas_call(
        functools.partial(_ret_kernel, seq=seq, hps=hps),
        grid=(batch, ng),
        in_specs=[
            pl.BlockSpec(memory_space=pltpu.SMEM),
            blk(0), blk(ng), blk(2 * ng), blk(3 * ng),
            tab(RET_CHUNK, RET_CHUNK), tab(RET_CHUNK, hd), tab(RET_CHUNK, hd),
        ],
        out_specs=[
            pl.BlockSpec((1, seq, hps * hd), lambda b, h: (b, 0, h)),
            pl.BlockSpec((1, hps, hd, hd), lambda b, h: (b, h, 0, 0)),
        ],
        out_shape=[
            jax.ShapeDtypeStruct((batch, seq, n_heads * hd), BF16),
            jax.ShapeDtypeStruct((batch, n_heads, hd, hd), F32),
        ],
        compiler_params=pltpu.CompilerParams(
            dimension_semantics=("arbitrary", "arbitrary"),
            vmem_limit_bytes=_vmem_limit(est + (16 << 20))),
        name="retention_prompt",
    )(gpow, z3, z3, z3, z3, dec, qd, kd)
    return o.reshape(batch * seq, n_heads * hd), st


def _bf(x):
    return x.astype(BF16).astype(F32)


def _sample_kernel(gam_ref, qkv_ref, zr_ref, k1_ref, k4_ref, k16_ref, v1_ref, v4_ref, v16_ref, st_ref,
                   oa_ref, or_ref, ns_ref, *, n_rh):
    rhd = RET_HEAD_DIM
    dr = n_rh * rhd
    qb = _bf(qkv_ref[0, 0])
    vnb = _bf(qkv_ref[0, 2])
    s_new = jnp.sum(qb * _bf(qkv_ref[0, 1]), axis=-1, keepdims=True)
    m_run = l_run = u_run = None
    for kb, vb in ((k1_ref, v1_ref), (k4_ref, v4_ref), (k16_ref, v16_ref)):
        s = jnp.sum(_bf(kb[0, :, 0]) * qb[None], axis=-1, keepdims=True)
        m = jnp.maximum(jnp.max(s, axis=0), s_new)
        p = jnp.exp(s - m[None])
        pn = jnp.exp(s_new - m)
        l = jnp.sum(p, axis=0) + pn
        u = jnp.sum(_bf(p) * _bf(vb[0, :, 0]), axis=0) + pn * vnb
        if m_run is None:
            m_run, l_run, u_run = m, l, u
        else:
            m_new = jnp.maximum(m_run, m)
            a_old = jnp.exp(m_run - m_new)
            a_new = jnp.exp(m - m_new)
            l_run = a_old * l_run + a_new * l
            u_run = a_old * u_run + a_new * u
            m_run = m_new
    oa_ref[0] = u_run / l_run

    zr = zr_ref[0]
    row0 = lax.broadcasted_iota(jnp.int32, (8, rhd), 0) == 0
    for h in range(n_rh):
        gam = gam_ref[h]
        q = zr[:, h * rhd:(h + 1) * rhd]
        k = zr[:, dr + h * rhd: dr + (h + 1) * rhd]
        v = zr[:, 2 * dr + h * rhd: 2 * dr + (h + 1) * rhd]
        g = zr[:, 3 * dr + h * rhd: 3 * dr + (h + 1) * rhd]
        st = st_ref[0, h]
        q8 = jnp.broadcast_to(q, (8, rhd)).astype(BF16)
        cross = _dot(q8, st.astype(BF16))[:1] * gam
        inner = _bf(jnp.sum(_bf(q) * _bf(k), axis=-1, keepdims=True))
        o = inner * _bf(v) + cross
        k8 = jnp.where(row0, jnp.broadcast_to(k, (8, rhd)), 0.0).astype(BF16)
        v8 = jnp.broadcast_to(v, (8, rhd)).astype(BF16)
        ns_ref[0, h] = gam * st + _dot_tn(k8, v8)
        o = o * lax.rsqrt(jnp.mean(o * o, axis=-1, keepdims=True) + RMS_EPS)
        or_ref[0, :, h * rhd:(h + 1) * rhd] = o * (g * jax.nn.sigmoid(g))


def _sample_mixer(za, zr, cache_k, cache_v, state, layer, *, n_ah, n_rh):
    nl, nb_, n_buf = cache_k.shape[:3]
    ahd, rhd = ATTN_HEAD_DIM, RET_HEAD_DIM
    da = n_ah * ahd
    dr = n_rh * rhd
    assert n_buf == DILATIONS[-1][0]
    span = DIL_BLOCK
    views, specs = [], []
    for cache in (cache_k, cache_v):
        for (_, r) in DILATIONS:
            views.append(cache.reshape(nl * nb_, n_buf // r, r, n_ah, ahd))
            rb = (n_buf // r) // span - 1
            specs.append(pl.BlockSpec((1, span, 1, n_ah, ahd), lambda b, rb=rb: (layer * nb_ + b, rb, 0, 0, 0)))
    ld = jnp.log1p(-jnp.exp2(-5.0 - jnp.arange(n_rh, dtype=F32)))
    gam = jnp.exp(ld)
    st4 = state.reshape(nl * nb_, n_rh, rhd, rhd)
    est = 2 * 6 * span * da * 4 + 4 * n_rh * rhd * rhd * 4 + 8 * span * n_ah * 128 * 4
    oa, orr, ns = pl.pallas_call(
        functools.partial(_sample_kernel, n_rh=n_rh),
        grid=(nb_,),
        in_specs=[
            pl.BlockSpec(memory_space=pltpu.SMEM),
            pl.BlockSpec((1, 3, n_ah, ahd), lambda b: (b, 0, 0, 0)),
            pl.BlockSpec((1, 1, 4 * dr), lambda b: (b, 0, 0)),
            *specs,
            pl.BlockSpec((1, n_rh, rhd, rhd), lambda b: (layer * nb_ + b, 0, 0, 0)),
        ],
        out_specs=[
            pl.BlockSpec((1, n_ah, ahd), lambda b: (b, 0, 0)),
            pl.BlockSpec((1, 1, dr), lambda b: (b, 0, 0)),
            pl.BlockSpec((1, n_rh, rhd, rhd), lambda b: (b, 0, 0, 0)),
        ],
        out_shape=[
            jax.ShapeDtypeStruct((nb_, n_ah, ahd), F32),
            jax.ShapeDtypeStruct((nb_, 1, dr), F32),
            jax.ShapeDtypeStruct((nb_, n_rh, rhd, rhd), F32),
        ],
        compiler_params=pltpu.CompilerParams(
            dimension_semantics=("arbitrary",),
            vmem_limit_bytes=_vmem_limit(est + (16 << 20))),
        name="sample_mixer",
    )(gam, za.reshape(nb_, 3, n_ah, ahd), zr.reshape(nb_, 1, 4 * dr), *views, st4)
    return jnp.concatenate([oa.reshape(nb_, da), orr.reshape(nb_, dr)], axis=-1), ns


def _rot_tables(pos, inv_freq, scales):
    ang = pos.astype(F32)[:, None] * inv_freq[None, :]
    cos, sin = jnp.cos(ang), jnp.sin(ang)
    c_full = jnp.concatenate([cos, cos], axis=-1)
    s_full = jnp.concatenate([-sin, sin], axis=-1)
    cs, ss = [], []
    for sc in scales:
        if sc is None:
            cs.append(jnp.ones_like(c_full))
            ss.append(jnp.zeros_like(s_full))
        else:
            cs.append(c_full * sc)
            ss.append(s_full * sc)
    return jnp.stack(cs), jnp.stack(ss)


def _tables(pos):
    inv_a = ROPE_THETA ** (-jnp.arange(0, ATTN_HEAD_DIM, 2, dtype=F32) / ATTN_HEAD_DIM)
    inv_r = 1.0 / (10000.0 ** jnp.linspace(0.0, 1.0, RET_HEAD_DIM // 2, dtype=F32))
    ta = _rot_tables(pos, inv_a, (ATTN_HEAD_DIM ** -0.5, 1.0, None))
    tr = _rot_tables(pos, inv_r, (1.0, RET_HEAD_DIM ** -0.5, None, None))
    return ta, tr


def kernel(x_prompt, x_sample, cache_attn_k, cache_attn_v, state_ret, g_ffn1, w_ffn1_gate, w_ffn1_up,
           w_ffn1_down, g_mix, w_in, w_out, g_ffn2, w_ffn2_gate, w_ffn2_up, w_ffn2_down, g_final):
    batch, seq, d = x_prompt.shape
    dec_batch, t_new, _ = x_sample.shape
    assert t_new == 1 and dec_batch <= SAMPLE_ROWS
    depth = g_ffn1.shape[0]
    d_attn = d // 2
    d_ret = d // 2
    n_ah = d_attn // ATTN_HEAD_DIM
    n_rh = d_ret // RET_HEAD_DIM
    m = batch * seq
    bm = min(1024, seq)
    ms = SAMPLE_ROWS
    assert seq == DILATIONS[-1][0]

    tabs_a, tabs_r = _tables(jnp.arange(seq))
    tabs_as, tabs_rs = _tables(jnp.full((ms,), PAST_LEN))
    bm_proj = min(512, seq)
    proj_a = dict(col0=0, n=3 * d_attn, hd=ATTN_HEAD_DIM, seg=d_attn, bm=bm_proj)
    proj_r = dict(col0=3 * d_attn, n=4 * d_ret, hd=RET_HEAD_DIM, seg=d_ret, bm=bm_proj)

    xp = x_prompt.reshape(m, d)
    xs = jnp.pad(x_sample.reshape(dec_batch, d), ((0, ms - dec_batch), (0, 0)))
    zas_p, ps, sk, sv, ss = [], [], [], [], []
    for l in range(depth):
        xp, hp, xs, hs = _ffn(xp, xs, g_ffn1[l], w_ffn1_gate, w_ffn1_up, w_ffn1_down, g_mix[l],
                              layer=l, mode="norm", bm=bm)
        za, za_s = _proj(hp, hs, w_in, tabs_a, tabs_as, layer=l, **proj_a)
        zr, zr_s = _proj(hp, hs, w_in, tabs_r, tabs_rs, layer=l, **proj_r)
        oa = _attention(za, batch=batch, seq=seq, n_heads=n_ah)
        orr, st_p = _retention(zr, batch=batch, seq=seq, n_heads=n_rh)
        o_s, st_s = _sample_mixer(za_s[:dec_batch], zr_s[:dec_batch], cache_attn_k, cache_attn_v, state_ret, l,
                                  n_ah=n_ah, n_rh=n_rh)
        o_s = jnp.pad(o_s, ((0, ms - dec_batch), (0, 0))).astype(BF16)
        xp, xs = _out_proj(xp, oa, orr, xs, o_s, w_out, layer=l, bm=bm_proj)
        last = l == depth - 1
        xp, xs = _ffn(xp, xs, g_ffn2[l], w_ffn2_gate, w_ffn2_up, w_ffn2_down, g_final,
                      layer=l, mode="final" if last else "mid", bm=bm)
        zas_p.append(za)
        ps.append(st_p)
        za_s4 = za_s[:dec_batch].reshape(dec_batch, 1, 3, n_ah, ATTN_HEAD_DIM)
        sk.append(za_s4[:, :, 1])
        sv.append(za_s4[:, :, 2])
        ss.append(st_s)

    pk, pv = _kv_outputs(zas_p, n_heads=n_ah)
    kv_shape = (depth, batch, seq, n_ah, ATTN_HEAD_DIM)
    y_prompt = xp.reshape(batch, seq, d)
    y_sample = xs[:dec_batch].reshape(dec_batch, t_new, d)
    return (y_prompt, y_sample, pk.reshape(kv_shape), pv.reshape(kv_shape), jnp.stack(ps),
            jnp.stack(sk), jnp.stack(sv), jnp.stack(ss))
```

```python
import functools

import jax
import jax.numpy as jnp
from jax import lax
from jax.experimental import pallas as pl
from jax.experimental.pallas import tpu as pltpu

F32 = jnp.float32
BF16 = jnp.bfloat16

ATTN_HEAD_DIM = 128
RET_HEAD_DIM = 256
DILATIONS = ((128, 1), (512, 4), (2048, 16))
DIL_BLOCK = 128
RET_CHUNK = 128
ROPE_THETA = 10000.0
RMS_EPS = 1e-6
NEG_INF = -1e30
PAST_LEN = 8192

V7X_VMEM_BYTES = 64 * 1024 * 1024
VMEM_HEADROOM_BYTES = 3 * 1024 * 1024
SAMPLE_ROWS = 16


def _vmem_limit(estimate_bytes):
    return int(min(V7X_VMEM_BYTES - VMEM_HEADROOM_BYTES, max(estimate_bytes, 16 * 1024 * 1024)))


def _rms(x, g):
    return x * lax.rsqrt(jnp.mean(x * x, axis=-1, keepdims=True) + RMS_EPS) * g


def _dot(a, b):
    return jnp.dot(a, b, preferred_element_type=F32)


def _dot_nt(a, b):
    return lax.dot_general(a, b, (((1,), (1,)), ((), ())), preferred_element_type=F32)


def _dot_tn(a, b):
    return lax.dot_general(a, b, (((0,), (0,)), ((), ())), preferred_element_type=F32)


FFN_LOAD_PARTS = 4


def _ffn_kernel(x_hbm, xs_ref, g_ref, wg_ref, wu_ref, wd_ref, gn_ref, *rest, bm, ms, mode, row_chunk, col_chunk):
    if mode == "norm":
        y_hbm, hn_hbm, ys_ref, hns_ref, acc, h_scr, sem = rest
    else:
        y_hbm, ys_ref, acc, h_scr, sem = rest
    i = pl.program_id(0)
    j = pl.program_id(1)
    nj = pl.num_programs(1)
    rows = pl.ds(pl.multiple_of(i * bm, bm), bm)
    d = acc.shape[1]
    part = bm // FFN_LOAD_PARTS
    tile = pl.ds(0, bm)
    extra = pl.ds(bm, ms)

    def for_row_chunks(fn, start, count):
        def body(t, carry):
            fn(pl.ds(pl.multiple_of(start + t * row_chunk, row_chunk), row_chunk))
            return carry
        lax.fori_loop(0, count // row_chunk, body, 0)

    @pl.when(j == 0)
    def _load():
        copies = [
            pltpu.make_async_copy(x_hbm.at[pl.ds(pl.multiple_of(i * bm + p * part, part), part)],
                                  acc.at[pl.ds(p * part, part)], sem.at[p])
            for p in range(FFN_LOAD_PARTS)]
        for cp in copies:
            cp.start()

        @pl.when(i == 0)
        def _():
            xs = xs_ref[...]
            acc[extra, :] = xs
            h_scr[extra, :] = _rms(xs, g_ref[...]).astype(BF16)

        def norm_rows(r):
            h_scr[r, :] = _rms(acc[r, :], g_ref[...]).astype(BF16)
        for p, cp in enumerate(copies):
            cp.wait()
            for_row_chunks(norm_rows, p * part, part)

    def step(n_rows):
        r = pl.ds(0, n_rows)
        h = h_scr[r, :]
        gate = _dot(h, wg_ref[...].astype(BF16))
        up = _dot(h, wu_ref[...].astype(BF16))
        a = ((0.5 * gate) * jax.nn.sigmoid(gate) * up).astype(BF16)
        wd = wd_ref[...].astype(BF16)
        for c0 in range(0, d, col_chunk):
            acc[r, c0:c0 + col_chunk] += _dot(a, wd[:, c0:c0 + col_chunk])

    @pl.when(i == 0)
    def _with_sample_rows():
        step(bm + ms)

    @pl.when(i > 0)
    def _prompt_rows_only():
        step(bm)

    @pl.when(j == nj - 1)
    def _store():
        @pl.when(i == 0)
        def _():
            ys = acc[extra, :]
            if mode == "final":
                ys_ref[...] = _rms(ys, gn_ref[...])
            else:
                ys_ref[...] = ys
            if mode == "norm":
                hns_ref[...] = _rms(ys, gn_ref[...]).astype(BF16)

        if mode == "final":
            def fin_rows(r):
                acc[r, :] = _rms(acc[r, :], gn_ref[...])
            for_row_chunks(fin_rows, 0, bm)
            cp = pltpu.make_async_copy(acc.at[tile], y_hbm.at[rows], sem.at[0])
            cp.start()
            cp.wait()
        else:
            cp = pltpu.make_async_copy(acc.at[tile], y_hbm.at[rows], sem.at[0])
            cp.start()
            if mode == "norm":
                def nrm_rows(r):
                    h_scr[r, :] = _rms(acc[r, :], gn_ref[...]).astype(BF16)
                for_row_chunks(nrm_rows, 0, bm)
                cp2 = pltpu.make_async_copy(h_scr.at[tile], hn_hbm.at[rows], sem.at[1])
                cp2.start()
                cp2.wait()
            cp.wait()


def _ffn(x, xs, g, wg, wu, wd, gn, *, layer, mode, bm, bn=256):
    m, d = x.shape
    ms = xs.shape[0]
    dff = wg.shape[2]
    row_chunk = min(bm // FFN_LOAD_PARTS, 64)
    assert m % bm == 0 and dff % bn == 0 and bm % (FFN_LOAD_PARTS * row_chunk) == 0
    col_chunk = min(d, 512)
    any_spec = pl.BlockSpec(memory_space=pl.ANY)
    small = pl.BlockSpec((ms, d), lambda i, j: (0, 0))
    vec = pl.BlockSpec((1, d), lambda i, j: (0, 0))
    out_shape = [jax.ShapeDtypeStruct((m, d), F32)]
    out_specs = [any_spec]
    if mode == "norm":
        out_shape.append(jax.ShapeDtypeStruct((m, d), BF16))
        out_specs.append(any_spec)
    out_shape.append(jax.ShapeDtypeStruct((ms, d), F32))
    out_specs.append(small)
    if mode == "norm":
        out_shape.append(jax.ShapeDtypeStruct((ms, d), BF16))
        out_specs.append(small)
    est = bm * d * 6 + 3 * 2 * d * bn * 4 + 3 * d * bn * 2 + 4 * bm * bn * 4 + bm * col_chunk * 4
    return pl.pallas_call(
        functools.partial(_ffn_kernel, bm=bm, ms=ms, mode=mode, row_chunk=row_chunk, col_chunk=col_chunk),
        grid=(m // bm, dff // bn),
        in_specs=[
            any_spec,
            small,
            vec,
            pl.BlockSpec((None, d, bn), lambda i, j: (layer, 0, j)),
            pl.BlockSpec((None, d, bn), lambda i, j: (layer, 0, j)),
            pl.BlockSpec((None, bn, d), lambda i, j: (layer, j, 0)),
            vec,
        ],
        out_specs=out_specs,
        out_shape=out_shape,
        scratch_shapes=[
            pltpu.VMEM((bm + ms, d), F32),
            pltpu.VMEM((bm + ms, d), BF16),
            pltpu.SemaphoreType.DMA((FFN_LOAD_PARTS,)),
        ],
        compiler_params=pltpu.CompilerParams(
            dimension_semantics=("arbitrary", "arbitrary"),
            vmem_limit_bytes=_vmem_limit(est + (4 << 20))),
        name=f"ffn_{mode}",
    )(x, xs, g.reshape(1, d), wg, wu, wd, gn.reshape(1, d))


def _next_chunk_weights(w_hbm, w_stage, w_scr, sem, *, layer, col_block0):
    j = pl.program_id(0)
    nj = pl.num_programs(0)
    bn = w_scr.shape[1]

    def w_copy(jj):
        cols = pl.ds(pl.multiple_of((col_block0 + jj) * bn, bn), bn)
        return pltpu.make_async_copy(w_hbm.at[layer, :, cols], w_stage, sem.at[0])

    @pl.when(j == 0)
    def _():
        w_copy(0).start()

    w_copy(j).wait()
    w_scr[...] = w_stage[...].astype(BF16)

    @pl.when(j + 1 < nj)
    def _():
        w_copy(j + 1).start()


def _proj_kernel(h_ref, hs_ref, w_hbm, c_ref, s_ref, cs_ref, ss_ref, o_ref, os_ref, w_stage, w_scr, sem,
                 *, hd, layer, col_block0):
    i = pl.program_id(1)

    def rotate_store(z, cc, ss, out):
        for c0 in range(0, z.shape[1], hd):
            zh = z[:, c0:c0 + hd]
            out[:, c0:c0 + hd] = zh * cc + pltpu.roll(zh, hd // 2, axis=1) * ss

    @pl.when(i == 0)
    def _first():
        _next_chunk_weights(w_hbm, w_stage, w_scr, sem, layer=layer, col_block0=col_block0)
        rotate_store(_dot(hs_ref[...], w_scr[...]), cs_ref[0], ss_ref[0], os_ref)

    rotate_store(_dot(h_ref[...], w_scr[...]), c_ref[0], s_ref[0], o_ref)


def _proj(h, hs, w, tabs, tabs_s, *, layer, col0, n, hd, seg, bm):
    m, d = h.shape
    ms = hs.shape[0]
    ctab, stab = tabs
    p = ctab.shape[1]
    bn = 1024 if seg % 1024 == 0 else 512
    assert m % bm == 0 and n % bn == 0 and col0 % bn == 0 and seg % bn == 0 and p % bm == 0
    npb = p // bm
    cb0 = col0 // bn
    spc = seg // bn
    est = 2 * bm * d * 2 + d * bn * 4 + d * bn * 2 + 4 * bm * bn * 4 + 4 * bm * hd * 4
    tab = pl.BlockSpec((1, bm, hd), lambda j, i: (j // spc, i % npb, 0))
    tab_s = pl.BlockSpec((1, ms, hd), lambda j, i: (j // spc, 0, 0))
    return pl.pallas_call(
        functools.partial(_proj_kernel, hd=hd, layer=layer, col_block0=cb0),
        grid=(n // bn, m // bm),
        in_specs=[
            pl.BlockSpec((bm, d), lambda j, i: (i, 0)),
            pl.BlockSpec((ms, d), lambda j, i: (0, 0)),
            pl.BlockSpec(memory_space=pl.ANY),
            tab, tab, tab_s, tab_s,
        ],
        out_specs=[
            pl.BlockSpec((bm, bn), lambda j, i: (i, j)),
            pl.BlockSpec((ms, bn), lambda j, i: (0, j)),
        ],
        out_shape=[jax.ShapeDtypeStruct((m, n), F32), jax.ShapeDtypeStruct((ms, n), F32)],
        scratch_shapes=[pltpu.VMEM((d, bn), F32), pltpu.VMEM((d, bn), BF16), pltpu.SemaphoreType.DMA((1,))],
        compiler_params=pltpu.CompilerParams(
            dimension_semantics=("arbitrary", "arbitrary"),
            vmem_limit_bytes=_vmem_limit(est + (4 << 20))),
        name=f"proj_rot_hd{hd}",
    )(h, hs, w, ctab, stab, *tabs_s)


def _out_kernel(x_ref, oa_ref, or_ref, xs_ref, os_ref, w_hbm, y_ref, ys_ref, w_stage, w_scr, sem, *, layer):
    i = pl.program_id(1)
    ka = oa_ref.shape[1]

    @pl.when(i == 0)
    def _first():
        _next_chunk_weights(w_hbm, w_stage, w_scr, sem, layer=layer, col_block0=0)
        ys_ref[...] = xs_ref[...] + _dot(os_ref[...], w_scr[...])

    y_ref[...] = x_ref[...] + (_dot(oa_ref[...], w_scr[:ka, :]) + _dot(or_ref[...], w_scr[ka:, :]))


def _out_proj(x, oa, orr, xs, os_, w, *, layer, bm):
    m, d = x.shape
    ms = xs.shape[0]
    ka, kr = oa.shape[1], orr.shape[1]
    bn = 1024 if d % 1024 == 0 else 512
    assert m % bm == 0 and d % bn == 0 and w.shape[1] == ka + kr and os_.shape == (ms, ka + kr)
    est = 2 * bm * (ka + kr) * 2 + (ka + kr) * bn * 4 + (ka + kr) * bn * 2 + 6 * bm * bn * 4
    return pl.pallas_call(
        functools.partial(_out_kernel, layer=layer),
        grid=(d // bn, m // bm),
        in_specs=[
            pl.BlockSpec((bm, bn), lambda j, i: (i, j)),
            pl.BlockSpec((bm, ka), lambda j, i: (i, 0)),
            pl.BlockSpec((bm, kr), lambda j, i: (i, 0)),
            pl.BlockSpec((ms, bn), lambda j, i: (0, j)),
            pl.BlockSpec((ms, ka + kr), lambda j, i: (0, 0)),
            pl.BlockSpec(memory_space=pl.ANY),
        ],
        out_specs=[
            pl.BlockSpec((bm, bn), lambda j, i: (i, j)),
            pl.BlockSpec((ms, bn), lambda j, i: (0, j)),
        ],
        out_shape=[jax.ShapeDtypeStruct((m, d), F32), jax.ShapeDtypeStruct((ms, d), F32)],
        scratch_shapes=[pltpu.VMEM((ka + kr, bn), F32), pltpu.VMEM((ka + kr, bn), BF16),
                        pltpu.SemaphoreType.DMA((1,))],
        compiler_params=pltpu.CompilerParams(
            dimension_semantics=("arbitrary", "arbitrary"),
            vmem_limit_bytes=_vmem_limit(est + (4 << 20))),
        name="out_proj",
    )(x, oa, orr, xs, os_, w)


def _kv_kernel(*refs, n_layers, n_heads, hd):
    k_refs = refs[:n_layers]
    v_refs = refs[n_layers:2 * n_layers]
    ko_ref, vo_ref = refs[2 * n_layers:]
    layer = pl.program_id(0)
    bm = k_refs[0].shape[0]
    for l in range(n_layers):
        @pl.when(layer == l)
        def _(l=l):
            for h in range(n_heads):
                rows = pl.ds(h, bm, stride=n_heads)
                ko_ref[rows, :] = k_refs[l][:, h * hd:(h + 1) * hd]
                vo_ref[rows, :] = v_refs[l][:, h * hd:(h + 1) * hd]


def _kv_outputs(zas, *, n_heads, bm=256):
    n_layers = len(zas)
    m = zas[0].shape[0]
    hd = ATTN_HEAD_DIM
    da = n_heads * hd
    nm = m // bm
    assert m % bm == 0

    def in_spec(l, col):
        return pl.BlockSpec((bm, da), lambda ll, i, l=l: (jnp.where(ll == l, i, jnp.where(ll < l, 0, nm - 1)), col))

    out_spec = pl.BlockSpec((bm * n_heads, hd), lambda ll, i: (ll * nm + i, 0))
    out_sds = jax.ShapeDtypeStruct((n_layers * m * n_heads, hd), F32)
    return pl.pallas_call(
        functools.partial(_kv_kernel, n_layers=n_layers, n_heads=n_heads, hd=hd),
        grid=(n_layers, nm),
        in_specs=[in_spec(l, 1) for l in range(n_layers)] + [in_spec(l, 2) for l in range(n_layers)],
        out_specs=[out_spec, out_spec],
        out_shape=[out_sds, out_sds],
        compiler_params=pltpu.CompilerParams(
            dimension_semantics=("arbitrary", "arbitrary"),
            vmem_limit_bytes=_vmem_limit(2 * 2 * (n_layers + 1) * bm * da * 4 + (8 << 20))),
        name="kv_window_outputs",
    )(*zas, *zas)


def _attn_block(qb, kw, vw, first):
    s = _dot_nt(qb, kw)
    qi = lax.broadcasted_iota(jnp.int32, s.shape, 0)
    kj = lax.broadcasted_iota(jnp.int32, s.shape, 1)
    if first:
        mask = kj <= qi
    else:
        mask = (kj >= qi) & (kj <= qi + DIL_BLOCK)
    s = jnp.where(mask, s, NEG_INF)
    m = jnp.max(s, axis=-1, keepdims=True)
    p = jnp.where(mask, jnp.exp(s - m), 0.0)
    l = jnp.sum(p, axis=-1, keepdims=True)
    u = _dot(p.astype(BF16), vw)
    return u, m, l


def _attn_kernel(q_ref, k_ref, v_ref, o_ref, qa, ka, va, ua, ma, la, *, seq):
    blk = DIL_BLOCK
    hd = q_ref.shape[2]
    (_, r_near), (_, r_mid), (_, r_far) = DILATIONS
    assert r_near == 1 and r_far % r_mid == 0
    sub = r_far // r_mid
    seg = seq // r_mid

    def bf(x):
        return x.astype(BF16)

    def stats(qb, kw, vw, first):
        u, m, l = _attn_block(bf(qb), bf(kw), bf(vw), first)
        return u, jnp.broadcast_to(m, (blk, hd)), jnp.broadcast_to(l, (blk, hd))

    def merge(u_old, m_old, l_old, u, m, l):
        m_run = jnp.maximum(m_old, m)
        a_old = jnp.exp(m_old - m_run)
        a_new = jnp.exp(m - m_run)
        return a_old * u_old + a_new * u, m_run, a_old * l_old + a_new * l

    for c in range(r_mid):
        src = pl.ds(c, seg, stride=r_mid)
        dst = pl.ds(c * seg, seg)
        qa[dst, :] = q_ref[0, src, :]
        ka[dst, :] = k_ref[0, src, :]
        va[dst, :] = v_ref[0, src, :]

    n_far = (seq // r_far) // blk
    for c in range(r_mid):
        for e in range(sub):
            for n in range(n_far):
                q_rows = pl.ds(c * seg + e + sub * n * blk, blk, stride=sub)
                if n == 0:
                    k_rows = q_rows
                else:
                    k_rows = pl.ds(c * seg + e + sub * (n - 1) * blk, 2 * blk, stride=sub)
                u, m, l = stats(qa[q_rows, :], ka[k_rows, :], va[k_rows, :], n == 0)
                ua[q_rows, :] = u
                ma[q_rows, :] = m
                la[q_rows, :] = l

    n_mid = seg // blk
    for c in range(r_mid):
        for n in range(n_mid):
            q_rows = pl.ds(c * seg + n * blk, blk)
            k_rows = q_rows if n == 0 else pl.ds(c * seg + (n - 1) * blk, 2 * blk)
            u, m, l = stats(qa[q_rows, :], ka[k_rows, :], va[k_rows, :], n == 0)
            u, m, l = merge(ua[q_rows, :], ma[q_rows, :], la[q_rows, :], u, m, l)
            ua[q_rows, :] = u
            ma[q_rows, :] = m
            la[q_rows, :] = l

    ut, mt, lt = qa, ka, va
    for c in range(r_mid):
        src = pl.ds(c * seg, seg)
        dst = pl.ds(c, seg, stride=r_mid)
        ut[dst, :] = ua[src, :]
        mt[dst, :] = ma[src, :]
        lt[dst, :] = la[src, :]

    for n in range(seq // blk):
        q_rows = pl.ds(n * blk, blk)
        k_rows = q_rows if n == 0 else pl.ds((n - 1) * blk, 2 * blk)
        u, m, l = stats(q_ref[0, q_rows, :], k_ref[0, k_rows, :], v_ref[0, k_rows, :], n == 0)
        u, m, l = merge(ut[q_rows, :], mt[q_rows, :], lt[q_rows, :], u, m, l)
        o_ref[0, q_rows, :] = (u / l).astype(BF16)


def _attention(za, *, batch, seq, n_heads):
    hd = ATTN_HEAD_DIM
    z3 = za.reshape(batch, seq, 3 * n_heads * hd)
    est = 3 * 2 * seq * hd * 4 + 2 * seq * hd * 2 + 6 * seq * hd * 4
    o = pl.pallas_call(
        functools.partial(_attn_kernel, seq=seq),
        grid=(batch, n_heads),
        in_specs=[
            pl.BlockSpec((1, seq, hd), lambda b, h: (b, 0, h)),
            pl.BlockSpec((1, seq, hd), lambda b, h: (b, 0, n_heads + h)),
            pl.BlockSpec((1, seq, hd), lambda b, h: (b, 0, 2 * n_heads + h)),
        ],
        out_specs=pl.BlockSpec((1, seq, hd), lambda b, h: (b, 0, h)),
        out_shape=jax.ShapeDtypeStruct((batch, seq, n_heads * hd), BF16),
        scratch_shapes=[pltpu.VMEM((seq, hd), F32)] * 6,
        compiler_params=pltpu.CompilerParams(
            dimension_semantics=("arbitrary", "arbitrary"),
            vmem_limit_bytes=_vmem_limit(est + (32 << 20))),
        name="dilated_attn_prompt",
    )(z3, z3, z3)
    return o.reshape(batch * seq, n_heads * hd)


RET_HEADS_PER_STEP = 2


def _ret_kernel(gpow_ref, q_ref, k_ref, v_ref, g_ref, dec_ref, qd_ref, kd_ref, o_ref, st_ref, *, seq, hps):
    ck = RET_CHUNK
    hd = RET_HEAD_DIM
    h0 = pl.program_id(1) * hps
    st_ref[...] = jnp.zeros_like(st_ref)

    def chunk(t, carry):
        rows = pl.ds(pl.multiple_of(t * ck, ck), ck)
        for hh in range(hps):
            cols = slice(hh * hd, (hh + 1) * hd)
            q = q_ref[0, rows, cols].astype(BF16)
            k = k_ref[0, rows, cols]
            v = v_ref[0, rows, cols].astype(BF16)
            inner = _dot_nt(q, k.astype(BF16)) * dec_ref[hh]
            intra = _dot(inner.astype(BF16), v)
            st = st_ref[0, hh]
            cross = _dot(q, st.astype(BF16)) * qd_ref[hh]
            o = intra + cross
            kdec = (k * kd_ref[hh]).astype(BF16)
            st_ref[0, hh] = gpow_ref[h0 + hh] * st + _dot_tn(kdec, v)
            o = o * lax.rsqrt(jnp.mean(o * o, axis=-1, keepdims=True) + RMS_EPS)
            g = g_ref[0, rows, cols]
            o_ref[0, rows, cols] = (o * (g * jax.nn.sigmoid(g))).astype(BF16)
        return carry

    lax.fori_loop(0, seq // ck, chunk, 0, unroll=4)


def _ret_tables(n_heads, hd):
    hh = jnp.arange(n_heads, dtype=F32)
    ld = jnp.log1p(-jnp.exp2(-5.0 - hh))
    n = jnp.arange(RET_CHUNK, dtype=F32)
    diff = n[:, None] - n[None, :]
    dec = jnp.where(diff >= 0, jnp.exp(jnp.maximum(diff, 0.0)[None] * ld[:, None, None]), 0.0)
    qd = jnp.exp((n[None, :] + 1.0) * ld[:, None])
    kd = jnp.exp((RET_CHUNK - 1.0 - n)[None, :] * ld[:, None])
    qd = jnp.broadcast_to(qd[:, :, None], (n_heads, RET_CHUNK, hd))
    kd = jnp.broadcast_to(kd[:, :, None], (n_heads, RET_CHUNK, hd))
    gpow = jnp.exp(RET_CHUNK * ld)
    return ld, dec, qd, kd, gpow


def _retention(zr, *, batch, seq, n_heads):
    hd = RET_HEAD_DIM
    hps = RET_HEADS_PER_STEP if n_heads % RET_HEADS_PER_STEP == 0 else 1
    ng = n_heads // hps
    z3 = zr.reshape(batch, seq, 4 * n_heads * hd)
    _, dec, qd, kd, gpow = _ret_tables(n_heads, hd)
    blk = lambda off: pl.BlockSpec((1, seq, hps * hd), lambda b, h: (b, 0, off + h))
    tab = lambda r, c: pl.BlockSpec((hps, r, c), lambda b, h: (h, 0, 0))
    est = 4 * 2 * seq * hps * hd * 4 + 2 * seq * hps * hd * 2 + 2 * hps * hd * hd * 4
    o, st = pl.pallas_call(
        functools.partial(_ret_kernel, seq=seq, hps=hps),
        grid=(batch, ng),
        in_specs=[
            pl.BlockSpec(memory_space=pltpu.SMEM),
            blk(0), blk(ng), blk(2 * ng), blk(3 * ng),
            tab(RET_CHUNK, RET_CHUNK), tab(RET_CHUNK, hd), tab(RET_CHUNK, hd),
        ],
        out_specs=[
            pl.BlockSpec((1, seq, hps * hd), lambda b, h: (b, 0, h)),
            pl.BlockSpec((1, hps, hd, hd), lambda b, h: (b, h, 0, 0)),
        ],
        out_shape=[
            jax.ShapeDtypeStruct((batch, seq, n_heads * hd), BF16),
            jax.ShapeDtypeStruct((batch, n_heads, hd, hd), F32),
        ],
        compiler_params=pltpu.CompilerParams(
            dimension_semantics=("arbitrary", "arbitrary"),
            vmem_limit_bytes=_vmem_limit(est + (16 << 20))),
        name="retention_prompt",
    )(gpow, z3, z3, z3, z3, dec, qd, kd)
    return o.reshape(batch * seq, n_heads * hd), st


def _bf(x):
    return x.astype(BF16).astype(F32)


def _sample_kernel(gam_ref, qkv_ref, zr_ref, k1_ref, k4_ref, k16_ref, v1_ref, v4_ref, v16_ref, st_ref,
                   oa_ref, or_ref, ns_ref, *, n_rh):
    rhd = RET_HEAD_DIM
    dr = n_rh * rhd
    qb = _bf(qkv_ref[0, 0])
    vnb = _bf(qkv_ref[0, 2])
    s_new = jnp.sum(qb * _bf(qkv_ref[0, 1]), axis=-1, keepdims=True)
    m_run = l_run = u_run = None
    for kb, vb in ((k1_ref, v1_ref), (k4_ref, v4_ref), (k16_ref, v16_ref)):
        s = jnp.sum(_bf(kb[0, :, 0]) * qb[None], axis=-1, keepdims=True)
        m = jnp.maximum(jnp.max(s, axis=0), s_new)
        p = jnp.exp(s - m[None])
        pn = jnp.exp(s_new - m)
        l = jnp.sum(p, axis=0) + pn
        u = jnp.sum(_bf(p) * _bf(vb[0, :, 0]), axis=0) + pn * vnb
        if m_run is None:
            m_run, l_run, u_run = m, l, u
        else:
            m_new = jnp.maximum(m_run, m)
            a_old = jnp.exp(m_run - m_new)
            a_new = jnp.exp(m - m_new)
            l_run = a_old * l_run + a_new * l
            u_run = a_old * u_run + a_new * u
            m_run = m_new
    oa_ref[0] = u_run / l_run

    zr = zr_ref[0]
    row0 = lax.broadcasted_iota(jnp.int32, (8, rhd), 0) == 0
    for h in range(n_rh):
        gam = gam_ref[h]
        q = zr[:, h * rhd:(h + 1) * rhd]
        k = zr[:, dr + h * rhd: dr + (h + 1) * rhd]
        v = zr[:, 2 * dr + h * rhd: 2 * dr + (h + 1) * rhd]
        g = zr[:, 3 * dr + h * rhd: 3 * dr + (h + 1) * rhd]
        st = st_ref[0, h]
        q8 = jnp.broadcast_to(q, (8, rhd)).astype(BF16)
        cross = _dot(q8, st.astype(BF16))[:1] * gam
        inner = _bf(jnp.sum(_bf(q) * _bf(k), axis=-1, keepdims=True))
        o = inner * _bf(v) + cross
        k8 = jnp.where(row0, jnp.broadcast_to(k, (8, rhd)), 0.0).astype(BF16)
        v8 = jnp.broadcast_to(v, (8, rhd)).astype(BF16)
        ns_ref[0, h] = gam * st + _dot_tn(k8, v8)
        o = o * lax.rsqrt(jnp.mean(o * o, axis=-1, keepdims=True) + RMS_EPS)
        or_ref[0, :, h * rhd:(h + 1) * rhd] = o * (g * jax.nn.sigmoid(g))


def _sample_mixer(za, zr, cache_k, cache_v, state, layer, *, n_ah, n_rh):
    nl, nb_, n_buf = cache_k.shape[:3]
    ahd, rhd = ATTN_HEAD_DIM, RET_HEAD_DIM
    da = n_ah * ahd
    dr = n_rh * rhd
    assert n_buf == DILATIONS[-1][0]
    span = DIL_BLOCK
    views, specs = [], []
    for cache in (cache_k, cache_v):
        for (_, r) in DILATIONS:
            views.append(cache.reshape(nl * nb_, n_buf // r, r, n_ah, ahd))
            rb = (n_buf // r) // span - 1
            specs.append(pl.BlockSpec((1, span, 1, n_ah, ahd), lambda b, rb=rb: (layer * nb_ + b, rb, 0, 0, 0)))
    ld = jnp.log1p(-jnp.exp2(-5.0 - jnp.arange(n_rh, dtype=F32)))
    gam = jnp.exp(ld)
    st4 = state.reshape(nl * nb_, n_rh, rhd, rhd)
    est = 2 * 6 * span * da * 4 + 4 * n_rh * rhd * rhd * 4 + 8 * span * n_ah * 128 * 4
    oa, orr, ns = pl.pallas_call(
        functools.partial(_sample_kernel, n_rh=n_rh),
        grid=(nb_,),
        in_specs=[
            pl.BlockSpec(memory_space=pltpu.SMEM),
            pl.BlockSpec((1, 3, n_ah, ahd), lambda b: (b, 0, 0, 0)),
            pl.BlockSpec((1, 1, 4 * dr), lambda b: (b, 0, 0)),
            *specs,
            pl.BlockSpec((1, n_rh, rhd, rhd), lambda b: (layer * nb_ + b, 0, 0, 0)),
        ],
        out_specs=[
            pl.BlockSpec((1, n_ah, ahd), lambda b: (b, 0, 0)),
            pl.BlockSpec((1, 1, dr), lambda b: (b, 0, 0)),
            pl.BlockSpec((1, n_rh, rhd, rhd), lambda b: (b, 0, 0, 0)),
        ],
        out_shape=[
            jax.ShapeDtypeStruct((nb_, n_ah, ahd), F32),
            jax.ShapeDtypeStruct((nb_, 1, dr), F32),
            jax.ShapeDtypeStruct((nb_, n_rh, rhd, rhd), F32),
        ],
        compiler_params=pltpu.CompilerParams(
            dimension_semantics=("arbitrary",),
            vmem_limit_bytes=_vmem_limit(est + (16 << 20))),
        name="sample_mixer",
    )(gam, za.reshape(nb_, 3, n_ah, ahd), zr.reshape(nb_, 1, 4 * dr), *views, st4)
    return jnp.concatenate([oa.reshape(nb_, da), orr.reshape(nb_, dr)], axis=-1), ns


def _rot_tables(pos, inv_freq, scales):
    ang = pos.astype(F32)[:, None] * inv_freq[None, :]
    cos, sin = jnp.cos(ang), jnp.sin(ang)
    c_full = jnp.concatenate([cos, cos], axis=-1)
    s_full = jnp.concatenate([-sin, sin], axis=-1)
    cs, ss = [], []
    for sc in scales:
        if sc is None:
            cs.append(jnp.ones_like(c_full))
            ss.append(jnp.zeros_like(s_full))
        else:
            cs.append(c_full * sc)
            ss.append(s_full * sc)
    return jnp.stack(cs), jnp.stack(ss)


def _tables(pos):
    inv_a = ROPE_THETA ** (-jnp.arange(0, ATTN_HEAD_DIM, 2, dtype=F32) / ATTN_HEAD_DIM)
    inv_r = 1.0 / (10000.0 ** jnp.linspace(0.0, 1.0, RET_HEAD_DIM // 2, dtype=F32))
    ta = _rot_tables(pos, inv_a, (ATTN_HEAD_DIM ** -0.5, 1.0, None))
    tr = _rot_tables(pos, inv_r, (1.0, RET_HEAD_DIM ** -0.5, None, None))
    return ta, tr


def kernel(x_prompt, x_sample, cache_attn_k, cache_attn_v, state_ret, g_ffn1, w_ffn1_gate, w_ffn1_up,
           w_ffn1_down, g_mix, w_in, w_out, g_ffn2, w_ffn2_gate, w_ffn2_up, w_ffn2_down, g_final):
    batch, seq, d = x_prompt.shape
    dec_batch, t_new, _ = x_sample.shape
    assert t_new == 1 and dec_batch <= SAMPLE_ROWS
    depth = g_ffn1.shape[0]
    d_attn = d // 2
    d_ret = d // 2
    n_ah = d_attn // ATTN_HEAD_DIM
    n_rh = d_ret // RET_HEAD_DIM
    m = batch * seq
    bm = min(1024, seq)
    ms = SAMPLE_ROWS
    assert seq == DILATIONS[-1][0]

    tabs_a, tabs_r = _tables(jnp.arange(seq))
    tabs_as, tabs_rs = _tables(jnp.full((ms,), PAST_LEN))
    bm_proj = min(512, seq)
    proj_a = dict(col0=0, n=3 * d_attn, hd=ATTN_HEAD_DIM, seg=d_attn, bm=bm_proj)
    proj_r = dict(col0=3 * d_attn, n=4 * d_ret, hd=RET_HEAD_DIM, seg=d_ret, bm=bm_proj)

    xp = x_prompt.reshape(m, d)
    xs = jnp.pad(x_sample.reshape(dec_batch, d), ((0, ms - dec_batch), (0, 0)))
    zas_p, ps, sk, sv, ss = [], [], [], [], []
    for l in range(depth):
        xp, hp, xs, hs = _ffn(xp, xs, g_ffn1[l], w_ffn1_gate, w_ffn1_up, w_ffn1_down, g_mix[l],
                              layer=l, mode="norm", bm=bm)
        za, za_s = _proj(hp, hs, w_in, tabs_a, tabs_as, layer=l, **proj_a)
        zr, zr_s = _proj(hp, hs, w_in, tabs_r, tabs_rs, layer=l, **proj_r)
        oa = _attention(za, batch=batch, seq=seq, n_heads=n_ah)
        orr, st_p = _retention(zr, batch=batch, seq=seq, n_heads=n_rh)
        o_s, st_s = _sample_mixer(za_s[:dec_batch], zr_s[:dec_batch], cache_attn_k, cache_attn_v, state_ret, l,
                                  n_ah=n_ah, n_rh=n_rh)
        o_s = jnp.pad(o_s, ((0, ms - dec_batch), (0, 0))).astype(BF16)
        xp, xs = _out_proj(xp, oa, orr, xs, o_s, w_out, layer=l, bm=bm_proj)
        last = l == depth - 1
        xp, xs = _ffn(xp, xs, g_ffn2[l], w_ffn2_gate, w_ffn2_up, w_ffn2_down, g_final,
                      layer=l, mode="final" if last else "mid", bm=bm)
        zas_p.append(za)
        ps.append(st_p)
        za_s4 = za_s[:dec_batch].reshape(dec_batch, 1, 3, n_ah, ATTN_HEAD_DIM)
        sk.append(za_s4[:, :, 1])
        sv.append(za_s4[:, :, 2])
        ss.append(st_s)

    pk, pv = _kv_outputs(zas_p, n_heads=n_ah)
    kv_shape = (depth, batch, seq, n_ah, ATTN_HEAD_DIM)
    y_prompt = xp.reshape(batch, seq, d)
    y_sample = xs[:dec_batch].reshape(dec_batch, t_new, d)
    return (y_prompt, y_sample, pk.reshape(kv_shape), pv.reshape(kv_shape), jnp.stack(ps),
            jnp.stack(sk), jnp.stack(sv), jnp.stack(ss))
```
